```python
import math
import jax, jax.numpy as jnp
from jax import lax
import numpy as np

D_MODEL = 1024
BATCH = 16
SEQ = 2048
DEPTH = 4
DEC_BATCH = 128
DEC_SEQ = 1
PAST_LEN = 8192
PAGE_SIZE = 128

N_EVEN = (DEPTH + 1) // 2
N_ODD = DEPTH // 2
EPS = 1e-6
Q_BLOCK = 128
NUM_BUCKETS = 32
MAX_DISTANCE = 128
H_A = 8
HD_A = 64
H_A_KV = 2
H_IDX = 4
HD_IDX = 64
TOPK_MAX = 256
H_B = 8
NOPE_B = 64
ROPE_B = 32
V_B = 64
Q_LORA = 256
KV_LORA = 128
ROPE_THETA = 10000.0
MLA_SCALE = (NOPE_B + ROPE_B) ** -0.5
H_C = 8
HD_C = 64
H_C_KV = 2
D_FF = -(-8 * D_MODEL // (3 * 256)) * 256

AB_SIZES = (H_A * HD_A, H_A_KV * HD_A, H_A_KV * HD_A, H_IDX * HD_IDX, HD_IDX, H_IDX, Q_LORA, KV_LORA, ROPE_B)
D_IN_AB = sum(AB_SIZES)
D_MIX_AB = H_A * HD_A + H_B * V_B
C_SIZES = (H_C * 2 * HD_C, H_C_KV * 2 * HD_C, H_C_KV * 2 * HD_C)
D_IN_C = sum(C_SIZES)
D_MIX_C = H_C * 2 * HD_C
LAT_DIM = KV_LORA + ROPE_B

kernel_name = 'hybrid_dsa_mla_diffattn_decode_step'


def rmsnorm(x, g):
    xf = x.astype(jnp.float32)
    y = xf * lax.rsqrt(jnp.mean(xf * xf, axis=-1, keepdims=True) + EPS)
    return (y * g.astype(jnp.float32)).astype(x.dtype)


def split_cols(x, sizes):
    outs, off = [], 0
    for s in sizes:
        outs.append(x[..., off:off + s])
        off += s
    return outs


def rope(x, pos):
    half = x.shape[-1] // 2
    inv = ROPE_THETA ** (-jnp.arange(half, dtype=jnp.float32) / half)
    ang = pos.astype(jnp.float32)[:, None] * inv[None, :]
    cos, sin = jnp.cos(ang)[None, :, None, :], jnp.sin(ang)[None, :, None, :]
    xf = x.astype(jnp.float32)
    x1, x2 = xf[..., :half], xf[..., half:]
    return jnp.concatenate([x1 * cos - x2 * sin, x2 * cos + x1 * sin], axis=-1).astype(x.dtype)


def t5_bias(table, rel):
    n = jnp.maximum(rel, 0)
    max_exact = NUM_BUCKETS // 2
    nf = jnp.maximum(n, 1).astype(jnp.float32)
    large = max_exact + (jnp.log(nf / max_exact) / math.log(MAX_DISTANCE / max_exact)
                         * (NUM_BUCKETS - max_exact)).astype(jnp.int32)
    bucket = jnp.where(n < max_exact, n, jnp.minimum(large, NUM_BUCKETS - 1))
    return table[bucket].astype(jnp.float32)


def over_query_blocks(fn, *args):
    b, t = args[0].shape[:2]
    if t <= Q_BLOCK or t % Q_BLOCK != 0:
        return fn(*args)
    nb = t // Q_BLOCK
    blocks = tuple(jnp.moveaxis(a.reshape((b, nb, Q_BLOCK) + a.shape[2:]), 1, 0) for a in args)
    out = lax.map(lambda xs: fn(*xs), blocks)
    return jnp.moveaxis(out, 0, 1).reshape((b, t) + out.shape[3:])


def gather_pages(cache, layer, page_table):
    rows = cache[layer, page_table]
    return rows.reshape((page_table.shape[0], page_table.shape[1] * PAGE_SIZE) + rows.shape[3:])


def indexer_select(i_q, i_w, keys_idx, q_pos, k_pos, topk):
    s = jnp.einsum('bqhd,bsd->bqhs', i_q, keys_idx, preferred_element_type=jnp.float32) * HD_IDX ** -0.5
    score = jnp.einsum('bqhs,bqh->bqs', jax.nn.relu(s), i_w.astype(jnp.float32))
    admissible = k_pos[None, None, :] <= q_pos[:, :, None]
    score = jnp.where(admissible, score, -jnp.inf)
    _, idx = lax.top_k(score, topk)
    return idx, idx <= q_pos[:, :, None]


def sparse_attn(q, kv_sel, idx, valid, q_pos, table):
    b, tq = q.shape[:2]
    g = H_A // H_A_KV
    kk = idx.shape[-1]
    qg = q.reshape(b, tq, H_A_KV, g, HD_A)
    logits = jnp.einsum('bqgrd,bqkgd->bqgrk', qg, kv_sel[:, :, :, 0],
                        preferred_element_type=jnp.float32) * HD_A ** -0.5
    bias = t5_bias(table, q_pos[:, :, None] - idx).reshape(b, tq, kk, H_A_KV, g)
    logits = logits + jnp.transpose(bias, (0, 1, 3, 4, 2))
    logits = jnp.where(valid[:, :, None, None, :], logits, -jnp.inf)
    p = jax.nn.softmax(logits, axis=-1).astype(q.dtype)
    o = jnp.einsum('bqgrk,bqkgd->bqgrd', p, kv_sel[:, :, :, 1])
    return o.reshape(b, tq, H_A * HD_A)


def mla_attn(q_lat, keys_lat, q_pos, k_pos, w_uv):
    b, tq = q_lat.shape[:2]
    logits = jnp.einsum('bqhc,bsc->bhqs', q_lat, keys_lat, preferred_element_type=jnp.float32) * MLA_SCALE
    mask = k_pos[None, None, :] <= q_pos[:, :, None]
    logits = jnp.where(mask[:, None], logits, -jnp.inf)
    p = jax.nn.softmax(logits, axis=-1).astype(q_lat.dtype)
    o_lat = jnp.einsum('bhqs,bsc->bqhc', p, keys_lat[..., :KV_LORA])
    return jnp.einsum('bqhc,chv->bqhv', o_lat, w_uv).reshape(b, tq, H_B * V_B)


def diff_attn(q, keys, q_pos, k_pos, lam, table):
    b, tq = q.shape[:2]
    s = keys.shape[1]
    g = H_C // H_C_KV
    qg = q.reshape(b, tq, H_C_KV, g, 2, HD_C)
    kk = keys[:, :, 0].reshape(b, s, H_C_KV, 2, HD_C)
    vv = keys[:, :, 1]
    logits = jnp.einsum('bqgrmd,bsgmd->bgrmqs', qg, kk, preferred_element_type=jnp.float32) * HD_C ** -0.5
    bias = t5_bias(table, q_pos[:, :, None] - k_pos[None, None, :]).reshape(b, tq, s, H_C_KV, g)
    logits = logits + jnp.transpose(bias, (0, 3, 4, 1, 2))[:, :, :, None]
    mask = k_pos[None, None, :] <= q_pos[:, :, None]
    logits = jnp.where(mask[:, None, None, None], logits, -jnp.inf)
    p = jax.nn.softmax(logits, axis=-1)
    p_diff = (p[:, :, :, 0] - lam * p[:, :, :, 1]).astype(q.dtype)
    o = jnp.einsum('bgrqs,bsge->bqgre', p_diff, vv)
    return o.reshape(b, tq, H_C, 2 * HD_C)


def mixer_ab(h, pos, w_in, w_uq, g_q, g_kv, w_uk, w_uv, w_o, tab_a, past):
    b, t, _ = h.shape
    a_q, a_k, a_v, i_q, i_k, i_w, c_q, c_kv, k_r = split_cols(h @ w_in, AB_SIZES)
    a_q = a_q.reshape(b, t, H_A, HD_A)
    a_kv = jnp.stack([a_k.reshape(b, t, H_A_KV, HD_A), a_v.reshape(b, t, H_A_KV, HD_A)], axis=2)
    i_q = i_q.reshape(b, t, H_IDX, HD_IDX)
    i_w = i_w * H_IDX ** -0.5
    q_b = (rmsnorm(c_q, g_q) @ w_uq).reshape(b, t, H_B, NOPE_B + ROPE_B)
    q_rope = rope(q_b[..., NOPE_B:], pos)
    q_lat = jnp.concatenate([jnp.einsum('bthn,chn->bthc', q_b[..., :NOPE_B], w_uk), q_rope], axis=-1)
    b_lat = jnp.concatenate([rmsnorm(c_kv, g_kv), rope(k_r[:, :, None, :], pos)[:, :, 0]], axis=-1)
    bi = jnp.arange(b)[:, None, None]
    if past is None:
        keys_idx, keys_lat, n_keys = i_k, b_lat, t

        def gather_kv(idx):
            return a_kv[bi, idx]
    else:
        c_akv, c_aidx, c_blat, e, page_table = past
        keys_idx = jnp.concatenate([gather_pages(c_aidx, e, page_table), i_k], axis=1)
        keys_lat = jnp.concatenate([gather_pages(c_blat, e, page_table), b_lat], axis=1)
        n_keys = PAST_LEN + t

        def gather_kv(idx):
            phys = page_table[bi, jnp.minimum(idx, PAST_LEN - 1) // PAGE_SIZE]
            past_rows = c_akv[e, phys, idx % PAGE_SIZE]
            new_rows = a_kv[bi, jnp.clip(idx - PAST_LEN, 0, t - 1)]
            return jnp.where((idx >= PAST_LEN)[..., None, None, None], new_rows, past_rows)
    k_pos = jnp.arange(n_keys, dtype=jnp.int32)
    topk = min(TOPK_MAX, n_keys // 4)

    def attend_block(aq, iq, iw, ql, qp):
        idx, valid = indexer_select(iq, iw, keys_idx, qp, k_pos, topk)
        o_a = sparse_attn(aq, gather_kv(idx), idx, valid, qp, tab_a)
        o_b = mla_attn(ql, keys_lat, qp, k_pos, w_uv)
        return jnp.concatenate([o_a, o_b], axis=-1)

    q_pos = jnp.broadcast_to(pos[None, :], (b, t))
    o = over_query_blocks(attend_block, a_q, i_q, i_w, q_lat, q_pos)
    return o @ w_o, a_kv, i_k, b_lat


def mixer_c(h, pos, layer, w_in, lam_qk, sub_g, w_o, tab_c, past):
    b, t, _ = h.shape
    q, k, v = split_cols(h @ w_in, C_SIZES)
    q = q.reshape(b, t, H_C, 2 * HD_C)
    c_kv = jnp.stack([k.reshape(b, t, H_C_KV, 2 * HD_C), v.reshape(b, t, H_C_KV, 2 * HD_C)], axis=2)
    lam_init = 0.8 - 0.6 * math.exp(-0.3 * layer)
    lq = lam_qk.astype(jnp.float32)
    lam = jnp.exp(jnp.sum(lq[0] * lq[1])) - jnp.exp(jnp.sum(lq[2] * lq[3])) + lam_init
    if past is None:
        keys = c_kv
    else:
        cache, j, page_table = past
        keys = jnp.concatenate([gather_pages(cache, j, page_table), c_kv], axis=1)
    k_pos = jnp.arange(keys.shape[1], dtype=jnp.int32)
    q_pos = jnp.broadcast_to(pos[None, :], (b, t))
    o = over_query_blocks(lambda qb, qp: diff_attn(qb, keys, qp, k_pos, lam, tab_c), q, q_pos)
    o = rmsnorm(o, sub_g) * (1.0 - lam_init)
    return o.reshape(b, t, D_MIX_C) @ w_o, c_kv


def swiglu(h, w_gate_up, w_down):
    gu = h @ w_gate_up
    return (jax.nn.silu(gu[..., :D_FF]) * gu[..., D_FF:]) @ w_down


def setup_inputs(seed: int = 0) -> dict:
    key = jax.random.key(seed)
    ks = jax.random.split(key, 24)
    n_pages = PAST_LEN // PAGE_SIZE
    pool = (DEC_BATCH * n_pages * 5) // 4

    def nrm(k, shape, scale=1.0):
        return scale * jax.random.normal(k, shape, jnp.float32)

    page_table = jax.random.permutation(ks[6], pool)[:DEC_BATCH * n_pages]
    page_table = page_table.reshape(DEC_BATCH, n_pages).astype(jnp.int32)
    return {
        'x_prompt': nrm(ks[0], (BATCH, SEQ, D_MODEL)),
        'x_sample': nrm(ks[1], (DEC_BATCH, DEC_SEQ, D_MODEL)),
        'cache_a_kv': nrm(ks[2], (N_EVEN, pool, PAGE_SIZE, 2, H_A_KV, HD_A)),
        'cache_a_idx': nrm(ks[3], (N_EVEN, pool, PAGE_SIZE, HD_IDX)),
        'cache_b_latent': nrm(ks[4], (N_EVEN, pool, PAGE_SIZE, LAT_DIM)),
        'cache_c_kv': nrm(ks[5], (N_ODD, pool, PAGE_SIZE, 2, H_C_KV, 2 * HD_C)),
        'page_table': page_table,
        'attn_norm': 1.0 + nrm(ks[7], (DEPTH, D_MODEL), 0.02),
        'w_in_ab': nrm(ks[8], (N_EVEN, D_MODEL, D_IN_AB), D_MODEL ** -0.5),
        'w_uq': nrm(ks[9], (N_EVEN, Q_LORA, H_B * (NOPE_B + ROPE_B)), Q_LORA ** -0.5),
        'g_q': 1.0 + nrm(ks[10], (N_EVEN, Q_LORA), 0.02),
        'g_kv': 1.0 + nrm(ks[11], (N_EVEN, KV_LORA), 0.02),
        'w_uk': nrm(ks[12], (N_EVEN, KV_LORA, H_B, NOPE_B), KV_LORA ** -0.5),
        'w_uv': nrm(ks[13], (N_EVEN, KV_LORA, H_B, V_B), KV_LORA ** -0.5),
        'w_o_ab': nrm(ks[14], (N_EVEN, D_MIX_AB, D_MODEL), D_MIX_AB ** -0.5),
        'rel_bias': nrm(ks[15], (NUM_BUCKETS, H_A + H_C), 0.5),
        'w_in_c': nrm(ks[16], (N_ODD, D_MODEL, D_IN_C), D_MODEL ** -0.5),
        'lambda_qk': nrm(ks[17], (N_ODD, 4, HD_C), 0.1),
        'sub_norm_c': 1.0 + nrm(ks[18], (N_ODD, 2 * HD_C), 0.02),
        'w_o_c': nrm(ks[19], (N_ODD, D_MIX_C, D_MODEL), D_MIX_C ** -0.5),
        'ffn_norm': 1.0 + nrm(ks[20], (DEPTH, D_MODEL), 0.02),
        'w_gate_up': nrm(ks[21], (DEPTH, D_MODEL, 2 * D_FF), D_MODEL ** -0.5),
        'w_down': nrm(ks[22], (DEPTH, D_FF, D_MODEL), D_FF ** -0.5),
        'final_norm': 1.0 + nrm(ks[23], (D_MODEL,), 0.02),
    }


def reference(x_prompt, x_sample, cache_a_kv, cache_a_idx, cache_b_latent, cache_c_kv, page_table,
              attn_norm, w_in_ab, w_uq, g_q, g_kv, w_uk, w_uv, w_o_ab, rel_bias,
              w_in_c, lambda_qk, sub_norm_c, w_o_c, ffn_norm, w_gate_up, w_down, final_norm):
    tab_a = rel_bias[:, :H_A]
    tab_c = rel_bias[:, H_A:]

    def run_group(x, pos0, sample):
        t = x.shape[1]
        pos = pos0 + jnp.arange(t, dtype=jnp.int32)
        a_kv_l, a_idx_l, b_lat_l, c_kv_l = [], [], [], []
        for l in range(DEPTH):
            h = rmsnorm(x, attn_norm[l])
            if l % 2 == 0:
                e = l // 2
                past = (cache_a_kv, cache_a_idx, cache_b_latent, e, page_table) if sample else None
                o, a_kv, a_idx, b_lat = mixer_ab(h, pos, w_in_ab[e], w_uq[e], g_q[e], g_kv[e], w_uk[e],
                                                 w_uv[e], w_o_ab[e], tab_a, past)
                a_kv_l.append(a_kv)
                a_idx_l.append(a_idx)
                b_lat_l.append(b_lat)
            else:
                j = l // 2
                past = (cache_c_kv, j, page_table) if sample else None
                o, c_kv = mixer_c(h, pos, l, w_in_c[j], lambda_qk[j], sub_norm_c[j], w_o_c[j], tab_c, past)
                c_kv_l.append(c_kv)
            x = x + o
            x = x + swiglu(rmsnorm(x, ffn_norm[l]), w_gate_up[l], w_down[l])
        return (rmsnorm(x, final_norm), jnp.stack(a_kv_l), jnp.stack(a_idx_l),
                jnp.stack(b_lat_l), jnp.stack(c_kv_l))

    y_p, akv_p, aidx_p, blat_p, ckv_p = run_group(x_prompt, 0, False)
    y_s, akv_s, aidx_s, blat_s, ckv_s = run_group(x_sample, PAST_LEN, True)
    return (y_p, y_s, akv_p, aidx_p, blat_p, ckv_p, akv_s, aidx_s, blat_s, ckv_s)
```

```python
import functools
import math

import numpy as np
import jax
import jax.numpy as jnp
from jax import lax
from jax.experimental import pallas as pl
from jax.experimental.pallas import tpu as pltpu

EPS = 1e-6
NUM_BUCKETS = 32
MAX_DISTANCE = 128
H_A, HD_A, H_A_KV = 8, 64, 2
H_IDX, HD_IDX = 4, 64
TOPK_MAX = 256
H_B, NOPE_B, ROPE_B, V_B = 8, 64, 32, 64
Q_LORA, KV_LORA = 256, 128
ROPE_THETA = 10000.0
MLA_SCALE = (NOPE_B + ROPE_B) ** -0.5
H_C, HD_C, H_C_KV = 8, 64, 2

LANES = 128
V7X_VMEM_BYTES = 64 * 2 ** 20

Q_BLOCK = LANES
F32 = jnp.float32
BF16 = jnp.bfloat16
NEG_INF = float("-inf")
INT_MIN = -2 ** 31


def _vmem_limit(nbytes):
    return int(min(max(nbytes, 32 * 2 ** 20), V7X_VMEM_BYTES - 8 * 2 ** 20))


def _params(semantics, vmem_bytes):
    return pltpu.CompilerParams(dimension_semantics=semantics, vmem_limit_bytes=_vmem_limit(vmem_bytes))


def _rms(x, g):
    return x * lax.rsqrt(jnp.mean(x * x, axis=-1, keepdims=True) + EPS) * g


def _dot(a, b):
    return jnp.dot(a, b, preferred_element_type=F32)


def _dot_nt(a, b):
    return lax.dot_general(a, b, (((1,), (1,)), ((), ())), preferred_element_type=F32)


def _const_spec(shape):
    nd = len(shape)
    return pl.BlockSpec(shape, lambda *_: (0,) * nd, pipeline_mode=pl.Buffered(1))


def _ordered_bits_to_float(u):
    key = jnp.maximum(u ^ jnp.int32(INT_MIN), jnp.int32(-0x7F800001))
    bits = jnp.where(key >= 0, key, key ^ jnp.int32(0x7FFFFFFF))
    return lax.bitcast_convert_type(bits, F32)


def _count(mask):
    return jnp.sum(jnp.where(mask, 1.0, 0.0), axis=1, keepdims=True)


def _topk_neg_mask(score, col, k):
    rows, s = score.shape
    kf = jnp.float32(k)

    def value_step(it, u):
        c = u | lax.shift_left(jnp.int32(1), 31 - it)
        return jnp.where(_count(score >= _ordered_bits_to_float(c)) >= kf, c, u)

    u = lax.fori_loop(0, 32, value_step, jnp.zeros((rows, 1), jnp.int32))
    thr = _ordered_bits_to_float(u)
    gt = score > thr
    eq = score == thr
    need = kf - _count(gt)
    n_eq = _count(eq)
    nbits = int(s).bit_length()

    def tie_search():
        eqf = jnp.where(eq, 1.0, 0.0)

        def idx_step(it, j):
            c = j | lax.shift_left(jnp.int32(1), nbits - 1 - it)
            n = jnp.sum(jnp.where(col < c, eqf, 0.0), axis=1, keepdims=True)
            return jnp.where(n <= need, c, j)

        return lax.fori_loop(0, nbits, idx_step, jnp.zeros((rows, 1), jnp.int32))

    overflow = jnp.max(n_eq - need) > 0.0
    jcut = lax.cond(overflow, tie_search, lambda: jnp.full((rows, 1), 2 ** nbits - 1, jnp.int32))
    keep_tie = jnp.where(col < jcut, 0.0, NEG_INF)
    return jnp.where(gt, 0.0, jnp.where(eq, keep_tie, NEG_INF))


AB_AQ, AB_AK, AB_AV, AB_IQ, AB_CQ, AB_CKV, AB_GRP, AB_COLS = 0, 512, 640, 768, 1024, 1280, 1408, 1536
GRP_KR, GRP_IW, GRP_IK = 0, 32, 64


def _ab_proj_kernel(x_ref, g_ref, w1_ref, gq_ref, w2_ref, wuk_ref, gkv_ref, cos_ref, sin_ref, nsl_ref, shi_ref,
                    aq_ref, akv_ref, iq_ref, ik_ref, iw_ref, qlat_ref, blat_ref, *t_refs, emit_t):
    h = _rms(x_ref[...], g_ref[...]).astype(BF16)
    y = _dot(h, w1_ref[...])
    aq_ref[...] = (y[:, AB_AQ:AB_AK] * HD_A ** -0.5).astype(BF16)
    akv_ref[...] = y[:, AB_AK:AB_IQ]
    iq_ref[...] = (y[:, AB_IQ:AB_CQ] * HD_IDX ** -0.5).astype(BF16)
    grp = y[:, AB_GRP:AB_COLS]
    ik_ref[...] = grp[:, GRP_IK:GRP_IK + HD_IDX]
    iw_ref[...] = grp[:, GRP_IW:GRP_IW + H_IDX] * H_IDX ** -0.5
    cq = _rms(y[:, AB_CQ:AB_CKV], gq_ref[...]).astype(BF16)
    q2 = _dot(cq, w2_ref[...])
    n_nope = H_B * NOPE_B
    n_pad = H_B * LANES
    qc = _dot(q2[:, :n_nope].astype(BF16), wuk_ref[...])
    cos8 = jnp.concatenate([cos_ref[...]] * H_B, axis=1)
    sin8 = jnp.concatenate([sin_ref[...]] * H_B, axis=1)
    qr = q2[:, n_nope:n_nope + n_pad] * cos8 + q2[:, n_nope + n_pad:] * sin8
    pieces = []
    for hh in range(H_B):
        pieces += [qc[:, hh * LANES:(hh + 1) * LANES], qr[:, hh * LANES:(hh + 1) * LANES]]
    qlat_ref[...] = (jnp.concatenate(pieces, axis=1) * MLA_SCALE).astype(BF16)
    latc = _rms(y[:, AB_CKV:AB_GRP], gkv_ref[...])
    half = ROPE_B // 2
    kr = (grp * cos_ref[...] + pltpu.roll(grp, LANES - half, 1) * nsl_ref[...]
          + pltpu.roll(grp, half, 1) * shi_ref[...])
    blat_ref[:, :KV_LORA] = latc
    blat_ref[:, KV_LORA:] = kr[:, :ROPE_B]
    if emit_t:
        av2_ref, latc_ref, akt_ref, ikt_ref, latt_ref = t_refs
        lane = lax.broadcasted_iota(jnp.int32, (y.shape[0], LANES), 1)
        v = y[:, AB_AV:AB_IQ]
        vs = pltpu.roll(v, HD_A, 1)
        av2_ref[...] = jnp.concatenate([jnp.where(lane < HD_A, v, vs), jnp.where(lane < HD_A, vs, v)],
                                       axis=1).astype(BF16)
        latc_ref[...] = latc.astype(BF16)
        kt = y[:, AB_AK:AB_AV].T
        akt_ref[...] = jnp.concatenate([kt[:HD_A], kt[:HD_A], kt[HD_A:], kt[HD_A:]], axis=0).astype(BF16)
        gt = grp.T
        ikt_ref[...] = jnp.concatenate([gt[GRP_IK:], gt[GRP_IK:]], axis=0).astype(BF16)
        latt_ref[...] = jnp.concatenate([latc.T, kr.T], axis=0).astype(BF16)


def _ab_proj(x, g, w1, gq, w2, wuk, gkv, rope_tabs, n_batch, t, emit_t):
    m = x.shape[0]
    tm = min(512, t) if emit_t else m
    assert m % tm == 0 and (not emit_t or t % tm == 0)
    tpb = max(t // tm, 1)
    row = lambda cols: pl.BlockSpec((tm, cols), lambda i: (i, 0))
    tab = pl.BlockSpec((tm, LANES), (lambda i: (i % tpb, 0)) if emit_t else (lambda i: (0, 0)))
    tsp = lambda rows: pl.BlockSpec((None, rows, tm), lambda i: (i // tpb, 0, i % tpb))
    out_shape = [jax.ShapeDtypeStruct((m, 512), BF16), jax.ShapeDtypeStruct((m, 256), F32),
                 jax.ShapeDtypeStruct((m, 256), BF16), jax.ShapeDtypeStruct((m, HD_IDX), F32),
                 jax.ShapeDtypeStruct((m, H_IDX), F32), jax.ShapeDtypeStruct((m, H_B * 256), BF16),
                 jax.ShapeDtypeStruct((m, KV_LORA + ROPE_B), F32)]
    out_specs = [row(512), row(256), row(256), row(HD_IDX), row(H_IDX), row(H_B * 256), row(KV_LORA + ROPE_B)]
    if emit_t:
        out_shape += [jax.ShapeDtypeStruct((m, 256), BF16), jax.ShapeDtypeStruct((m, KV_LORA), BF16),
                      jax.ShapeDtypeStruct((n_batch, 256, t), BF16), jax.ShapeDtypeStruct((n_batch, 128, t), BF16),
                      jax.ShapeDtypeStruct((n_batch, 256, t), BF16)]
        out_specs += [row(256), row(KV_LORA), tsp(256), tsp(128), tsp(256)]
    return pl.pallas_call(
        functools.partial(_ab_proj_kernel, emit_t=emit_t),
        grid=(m // tm,),
        in_specs=[row(x.shape[1]), _const_spec(g.shape), _const_spec(w1.shape), _const_spec(gq.shape),
                  _const_spec(w2.shape), _const_spec(wuk.shape), _const_spec(gkv.shape), tab, tab, tab, tab],
        out_specs=out_specs, out_shape=out_shape,
        compiler_params=_params(("parallel",), 40 * 2 ** 20),
        name="ab_proj_prompt" if emit_t else "ab_proj_sample",
    )(x, g, w1, gq, w2, wuk, gkv, *rope_tabs)


def _c_proj_kernel(x_ref, g_ref, w_ref, q_ref, ckv_ref, *t_refs, emit_t):
    h = _rms(x_ref[...], g_ref[...]).astype(BF16)
    y = _dot(h, w_ref[...])
    nq = H_C * 2 * HD_C
    nk = H_C_KV * 2 * HD_C
    q_ref[...] = (y[:, :nq] * HD_C ** -0.5).astype(BF16)
    ckv_ref[...] = y[:, nq:]
    if emit_t:
        ckt_ref, cv_ref = t_refs
        ckt_ref[...] = y[:, nq:nq + nk].T.astype(BF16)
        cv_ref[...] = y[:, nq + nk:].astype(BF16)


def _c_proj(x, g, w, n_batch, t, emit_t):
    m = x.shape[0]
    tm = min(512, t) if emit_t else m
    tpb = max(t // tm, 1)
    nq, nk = H_C * 2 * HD_C, H_C_KV * 2 * HD_C
    row = lambda cols: pl.BlockSpec((tm, cols), lambda i: (i, 0))
    out_shape = [jax.ShapeDtypeStruct((m, nq), BF16), jax.ShapeDtypeStruct((m, 2 * nk), F32)]
    out_specs = [row(nq), row(2 * nk)]
    if emit_t:
        out_shape += [jax.ShapeDtypeStruct((n_batch, nk, t), BF16), jax.ShapeDtypeStruct((m, nk), BF16)]
        out_specs += [pl.BlockSpec((None, nk, tm), lambda i: (i // tpb, 0, i % tpb)), row(nk)]
    return pl.pallas_call(
        functools.partial(_c_proj_kernel, emit_t=emit_t),
        grid=(m // tm,),
        in_specs=[row(x.shape[1]), _const_spec(g.shape), _const_spec(w.shape)],
        out_specs=out_specs, out_shape=out_shape,
        compiler_params=_params(("parallel",), 32 * 2 ** 20),
        name="c_proj_prompt" if emit_t else "c_proj_sample",
    )(x, g, w)


def _out_ffn_kernel(x_ref, o_ref, wo_ref, g_ref, wgu_ref, wd_ref, fg_ref, out_ref, *, d_ff, chunk, final):
    x1 = x_ref[...] + _dot(o_ref[...], wo_ref[...])
    h = _rms(x1, g_ref[...]).astype(BF16)
    acc = x1
    for c in range(d_ff // chunk):
        gate = _dot(h, wgu_ref[:, c * chunk:(c + 1) * chunk])
        up = _dot(h, wgu_ref[:, d_ff + c * chunk:d_ff + (c + 1) * chunk])
        act = gate * jax.nn.sigmoid(gate) * up
        acc = acc + _dot(act.astype(BF16), wd_ref[c * chunk:(c + 1) * chunk, :])
    if final:
        acc = _rms(acc, fg_ref[...])
    out_ref[...] = acc


def _out_ffn(x, o, wo, g, wgu, wd, fg, final):
    m, d = x.shape
    d_ff = wd.shape[0]
    chunk = 256 if d_ff % 256 == 0 else LANES
    assert d_ff % chunk == 0
    tm = min(512, m)
    assert m % tm == 0
    row = lambda cols: pl.BlockSpec((tm, cols), lambda i: (i, 0))
    return pl.pallas_call(
        functools.partial(_out_ffn_kernel, d_ff=d_ff, chunk=chunk, final=final),
        grid=(m // tm,),
        in_specs=[row(d), row(o.shape[1]), _const_spec(wo.shape), _const_spec(g.shape), _const_spec(wgu.shape),
                  _const_spec(wd.shape), _const_spec(fg.shape)],
        out_specs=row(d), out_shape=jax.ShapeDtypeStruct((m, d), F32),
        compiler_params=_params(("parallel",), 52 * 2 ** 20),
        name="out_ffn",
    )(x, o, wo, g, wgu, wd, fg)


def _mm_kernel(a_ref, b_ref, o_ref):
    o_ref[...] = _dot(a_ref[...], b_ref[...]).astype(o_ref.dtype)


def _mm(a, b, out_dtype):
    return pl.pallas_call(_mm_kernel, out_shape=jax.ShapeDtypeStruct((a.shape[0], b.shape[1]), out_dtype),
                          name="small_matmul")(a, b)


def _near_bias_rmw(l_ref, logits, near_ref, head, qb, s):
    nb = s // Q_BLOCK
    for j in range(nb):
        l_ref[j + 1] = logits[:, j * Q_BLOCK:(j + 1) * Q_BLOCK]
    l_ref[qb] = l_ref[qb] + near_ref[head, :, :Q_BLOCK]
    l_ref[qb + 1] = l_ref[qb + 1] + near_ref[head, :, Q_BLOCK:]
    return jnp.concatenate([l_ref[j + 1] for j in range(nb)], axis=1)


def _softmax_parts(logits):
    m = jnp.max(logits, axis=1, keepdims=True)
    p = jnp.exp(logits - m)
    return p.astype(BF16), 1.0 / jnp.sum(p, axis=1, keepdims=True)


def _attn_even_body(s, qb, topk, aq_ref, iq_ref, iw_ref, qlat_ref, akt_ref, av2_ref, ikt_ref, latt_ref, latc_ref,
                    near_ref, wuv_ref, o_ref, l_ref):
    row = lax.broadcasted_iota(jnp.int32, (Q_BLOCK, s), 0)
    col = lax.broadcasted_iota(jnp.int32, (Q_BLOCK, s), 1)
    adm = col <= qb * Q_BLOCK + row
    lo = lax.broadcasted_iota(jnp.int32, (Q_BLOCK, LANES), 1) < HD_A
    zero = jnp.zeros((), BF16)
    l_ref[0] = jnp.zeros((Q_BLOCK, Q_BLOCK), F32)
    iq = iq_ref[...]
    iw = iw_ref[...]
    ikt = ikt_ref[:, :s]
    score = jnp.zeros((Q_BLOCK, s), F32)
    for hh in range(H_IDX):
        grp = iq[:, (hh // 2) * LANES:(hh // 2 + 1) * LANES]
        qm = jnp.where(lo if hh % 2 == 0 else jnp.logical_not(lo), grp, zero)
        score = score + jnp.maximum(_dot(qm, ikt), 0.0) * iw[:, hh:hh + 1]
    score = jnp.where(adm, score, NEG_INF)
    neg_sel = jnp.where(adm, _topk_neg_mask(score, col, topk), NEG_INF)
    neg_causal = jnp.where(adm, 0.0, NEG_INF)
    aq = aq_ref[...]
    gsz = H_A // H_A_KV
    for g in range(H_A_KV):
        kt = akt_ref[g * LANES:(g + 1) * LANES, :s]
        vv = av2_ref[:s, g * LANES:(g + 1) * LANES]
        for pair in range(gsz // 2):
            grp = aq[:, (g * gsz // 2 + pair) * LANES:(g * gsz // 2 + pair + 1) * LANES]
            outs = []
            for e in range(2):
                head = g * gsz + 2 * pair + e
                qm = jnp.where(lo if e == 0 else jnp.logical_not(lo), grp, zero)
                logits = _near_bias_rmw(l_ref, _dot(qm, kt) + neg_sel, near_ref, head, qb, s)
                p, inv = _softmax_parts(logits)
                outs.append(_dot(p, vv) * inv)
            col0 = (g * gsz // 2 + pair) * LANES
            o_ref[:, col0:col0 + LANES] = jnp.where(lo, outs[0], outs[1]).astype(BF16)
    latt = latt_ref[:, :s]
    latc = latc_ref[:s, :]
    base = H_A * HD_A
    for pair in range(H_B // 2):
        acc = None
        for e in range(2):
            head = 2 * pair + e
            p, inv = _softmax_parts(_dot(qlat_ref[:, head * 256:(head + 1) * 256], latt) + neg_causal)
            olat = (_dot(p, latc) * inv).astype(BF16)
            part = _dot(olat, wuv_ref[head])
            acc = part if acc is None else acc + part
        o_ref[:, base + pair * LANES:base + (pair + 1) * LANES] = acc.astype(BF16)


def _attn_even_kernel(aq_ref, iq_ref, iw_ref, qlat_ref, akt_ref, av2_ref, ikt_ref, latt_ref, latc_ref, near_ref,
                      wuv_ref, o_ref, l_ref, *, t, n_cls, topk):
    qb = pl.program_id(1)
    per = (t // Q_BLOCK) // n_cls
    for c in range(n_cls):
        @pl.when(qb // per == c)
        def _(c=c):
            _attn_even_body((c + 1) * per * Q_BLOCK, qb, topk, aq_ref, iq_ref, iw_ref, qlat_ref, akt_ref, av2_ref,
                            ikt_ref, latt_ref, latc_ref, near_ref, wuv_ref, o_ref, l_ref)


def _key_classes(t):
    nqb = t // Q_BLOCK
    return 4 if nqb % 4 == 0 else 1


def _attn_even(aq, iq, iw, qlat, akt, av2, ikt, latt, latc, near, wuv, n_batch, t, topk):
    nqb = t // Q_BLOCK
    qrow = lambda cols: pl.BlockSpec((Q_BLOCK, cols), lambda b, i: (b * nqb + i, 0))
    krow = lambda cols: pl.BlockSpec((t, cols), lambda b, i: (b, 0))
    kcol = lambda rows: pl.BlockSpec((None, rows, t), lambda b, i: (b, 0, 0))
    return pl.pallas_call(
        functools.partial(_attn_even_kernel, t=t, n_cls=_key_classes(t), topk=topk),
        grid=(n_batch, nqb),
        in_specs=[qrow(512), qrow(256), qrow(H_IDX), qrow(H_B * 256), kcol(256), krow(256), kcol(128), kcol(256),
                  krow(KV_LORA), _const_spec(near.shape), _const_spec(wuv.shape)],
        out_specs=qrow(1024), out_shape=jax.ShapeDtypeStruct((n_batch * t, 1024), BF16),
        scratch_shapes=[pltpu.VMEM((nqb + 1, Q_BLOCK, Q_BLOCK), F32)],
        compiler_params=_params(("parallel", "arbitrary"), 48 * 2 ** 20),
        name="attn_even_prompt",
    )(aq, iq, iw, qlat, akt, av2, ikt, latt, latc, near, wuv)


def _attn_odd_body(s, qb, lam, lam_init, q_ref, ckt_ref, cv_ref, near_ref, subg_ref, o_ref, l_ref):
    row = lax.broadcasted_iota(jnp.int32, (Q_BLOCK, s), 0)
    col = lax.broadcasted_iota(jnp.int32, (Q_BLOCK, s), 1)
    neg_causal = jnp.where(col <= qb * Q_BLOCK + row, 0.0, NEG_INF)
    lo = lax.broadcasted_iota(jnp.int32, (Q_BLOCK, LANES), 1) < HD_C
    zero = jnp.zeros((), BF16)
    l_ref[0] = jnp.zeros((Q_BLOCK, Q_BLOCK), F32)
    gsz = H_C // H_C_KV
    for head in range(H_C):
        g = head // gsz
        kt = ckt_ref[g * LANES:(g + 1) * LANES, :s]
        v = cv_ref[:s, g * LANES:(g + 1) * LANES]
        grp = q_ref[:, head * LANES:(head + 1) * LANES]
        outs = []
        for e in range(2):
            qm = jnp.where(lo if e == 0 else jnp.logical_not(lo), grp, zero)
            logits = _near_bias_rmw(l_ref, _dot(qm, kt) + neg_causal, near_ref, head, qb, s)
            p, inv = _softmax_parts(logits)
            outs.append(_dot(p, v) * inv)
        o = outs[0] - lam * outs[1]
        o = _rms(o, subg_ref[...]) * (1.0 - lam_init)
        o_ref[:, head * LANES:(head + 1) * LANES] = o.astype(BF16)


def _attn_odd_kernel(lam_ref, q_ref, ckt_ref, cv_ref, near_ref, subg_ref, o_ref, l_ref, *, t, n_cls, lam_init):
    qb = pl.program_id(1)
    per = (t // Q_BLOCK) // n_cls
    lam = lam_ref[0]
    for c in range(n_cls):
        @pl.when(qb // per == c)
        def _(c=c):
            _attn_odd_body((c + 1) * per * Q_BLOCK, qb, lam, lam_init, q_ref, ckt_ref, cv_ref, near_ref, subg_ref,
                           o_ref, l_ref)


def _attn_odd(lam, q, ckt, cv, near, subg, n_batch, t, lam_init):
    nqb = t // Q_BLOCK
    qrow = lambda cols: pl.BlockSpec((Q_BLOCK, cols), lambda b, i: (b * nqb + i, 0))
    return pl.pallas_call(
        functools.partial(_attn_odd_kernel, t=t, n_cls=_key_classes(t), lam_init=lam_init),
        grid=(n_batch, nqb),
        in_specs=[pl.BlockSpec(memory_space=pltpu.SMEM), qrow(1024),
                  pl.BlockSpec((None, 256, t), lambda b, i: (b, 0, 0)), pl.BlockSpec((t, 256), lambda b, i: (b, 0)),
                  _const_spec(near.shape), _const_spec(subg.shape)],
        out_specs=qrow(1024), out_shape=jax.ShapeDtypeStruct((n_batch * t, 1024), BF16),
        scratch_shapes=[pltpu.VMEM((nqb + 1, Q_BLOCK, Q_BLOCK), F32)],
        compiler_params=_params(("parallel", "arbitrary"), 40 * 2 ** 20),
        name="attn_odd_prompt",
    )(lam, q, ckt, cv, near, subg)


PAGES_PER_STEP = 8


def _page_specs(block, layer, n):
    nd = len(block)

    def spec(j):
        return pl.BlockSpec((None, None) + block,
                            lambda b, g, pt: (layer, pt[b, g * n + j]) + (0,) * nd)

    return [spec(j) for j in range(n)]


def _dec_scores_kernel(pt_ref, iq_ref, iw_ref, *refs, n):
    pages, out_ref = refs[:n], refs[n]
    iq = iq_ref[...]
    iw = iw_ref[...]
    for j in range(n):
        sc = jnp.maximum(_dot(iq, pages[j][...].astype(BF16)), 0.0) * iw
        out_ref[:, j * LANES:(j + 1) * LANES] = jnp.sum(sc, axis=0, keepdims=True)


def _dec_scores(page_table, iq_s, iw_s, cache_idx_t, layer):
    nb, npg = page_table.shape
    n = PAGES_PER_STEP if npg % PAGES_PER_STEP == 0 else 1
    page = cache_idx_t.shape[3]
    assert page == LANES
    grid_spec = pltpu.PrefetchScalarGridSpec(
        num_scalar_prefetch=1, grid=(nb, npg // n),
        in_specs=[pl.BlockSpec((None, 8, HD_IDX), lambda b, g, pt: (b, 0, 0)),
                  pl.BlockSpec((None, 8, 1), lambda b, g, pt: (b, 0, 0))]
        + _page_specs((HD_IDX, page), layer, n),
        out_specs=pl.BlockSpec((None, 1, n * page), lambda b, g, pt: (b, 0, g)))
    return pl.pallas_call(
        functools.partial(_dec_scores_kernel, n=n), grid_spec=grid_spec,
        out_shape=jax.ShapeDtypeStruct((nb, 1, npg * page), F32),
        compiler_params=_params(("parallel", "arbitrary"), 32 * 2 ** 20),
        name="decode_index_scores",
    )(page_table, iq_s, iw_s, *([cache_idx_t] * n))


def _dec_select_kernel(sc_ref, iq_ref, ik_ref, iw_ref, neg_ref, negself_ref, *, topk):
    rows, past = sc_ref.shape
    iq = iq_ref[...].astype(F32)
    ik = ik_ref[...]
    iw = iw_ref[...]
    self_score = jnp.zeros((rows, 1), F32)
    for hh in range(H_IDX):
        s = jnp.sum(iq[:, hh * HD_IDX:(hh + 1) * HD_IDX] * ik, axis=1, keepdims=True)
        self_score = self_score + jnp.maximum(s, 0.0) * iw[:, hh:hh + 1]
    lane = lax.broadcasted_iota(jnp.int32, (rows, LANES), 1)
    tail = jnp.where(lane == 0, self_score, NEG_INF)
    score = jnp.concatenate([sc_ref[...], tail], axis=1)
    col = lax.broadcasted_iota(jnp.int32, score.shape, 1)
    neg = jnp.where(col <= past, _topk_neg_mask(score, col, topk), NEG_INF)
    neg_ref[...] = neg[:, :past]
    negself_ref[...] = neg[:, past:]


def _dec_select(scores, iq, ik, iw, topk):
    rows, past = scores.shape
    return pl.pallas_call(
        functools.partial(_dec_select_kernel, topk=topk),
        out_shape=[jax.ShapeDtypeStruct((rows, past), F32), jax.ShapeDtypeStruct((rows, LANES), F32)],
        compiler_params=pltpu.CompilerParams(vmem_limit_bytes=_vmem_limit(48 * 2 ** 20)),
        name="decode_topk_select",
    )(scores, iq, ik, iw)


def _flash_update(m_ref, s_ref, acc_ref, logits, pv_fn):
    m_old = m_ref[...]
    m_new = jnp.maximum(m_old, jnp.max(logits, axis=1, keepdims=True))
    m_safe = jnp.where(m_new == NEG_INF, 0.0, m_new)
    alpha = jnp.exp(m_old - m_safe)
    p = jnp.exp(logits - m_safe)
    s_ref[...] = alpha * s_ref[...] + jnp.sum(p, axis=1, keepdims=True)
    acc_ref[...] = alpha * acc_ref[...] + pv_fn(p.astype(BF16))
    m_ref[...] = m_new


def _flash_finish(m_ref, s_ref, acc_ref, l_self, v_self):
    m_old = m_ref[...]
    m_new = jnp.maximum(m_old, l_self)
    m_safe = jnp.where(m_new == NEG_INF, 0.0, m_new)
    alpha = jnp.exp(m_old - m_safe)
    p = jnp.exp(l_self - m_safe)
    return (alpha * acc_ref[...] + p * v_self) / (alpha * s_ref[...] + p)


def _add_near_last_page(logits, near, n):
    if n == 1:
        return logits + near
    return jnp.concatenate([logits[:, :(n - 1) * LANES], logits[:, (n - 1) * LANES:] + near], axis=1)


def _dec_even_kernel(pt_ref, neg_ref, negself_ref, aq_ref, qlat_ref, kvn_ref, latn_ref, near_ref, bself_ref,
                     *refs, n):
    kv_pages, lat_pages = refs[:n], refs[n:2 * n]
    oa_ref, ol_ref = refs[2 * n:2 * n + 2]
    ma_ref, sa_ref, acca_ref, mb_ref, sb_ref, accb_ref = refs[2 * n + 2:]
    g = pl.program_id(1)
    last = pl.num_programs(1) - 1

    @pl.when(g == 0)
    def _():
        ma_ref[...] = jnp.full(ma_ref.shape, NEG_INF, F32)
        mb_ref[...] = jnp.full(mb_ref.shape, NEG_INF, F32)
        sa_ref[...] = jnp.zeros(sa_ref.shape, F32)
        sb_ref[...] = jnp.zeros(sb_ref.shape, F32)
        acca_ref[...] = jnp.zeros(acca_ref.shape, F32)
        accb_ref[...] = jnp.zeros(accb_ref.shape, F32)

    aq = aq_ref[...]
    qlat = qlat_ref[...]
    kv = [p[...].astype(BF16) for p in kv_pages]
    la = jnp.concatenate([_dot(aq, kvp[:LANES]) for kvp in kv], axis=1) + neg_ref[...]
    la = _add_near_last_page(la, jnp.where(g == last, near_ref[...], 0.0), n)

    def pv_a(p):
        return sum(_dot_nt(p[:, j * LANES:(j + 1) * LANES], kv[j][LANES:]) for j in range(n))

    _flash_update(ma_ref, sa_ref, acca_ref, la, pv_a)
    lat = [p[...].astype(BF16) for p in lat_pages]
    zrows = jnp.zeros((256 - KV_LORA - ROPE_B, LANES), BF16)
    lb = jnp.concatenate([_dot(qlat, jnp.concatenate([lp, zrows], axis=0)) for lp in lat], axis=1)

    def pv_b(p):
        return sum(_dot_nt(p[:, j * LANES:(j + 1) * LANES], lat[j][:KV_LORA]) for j in range(n))

    _flash_update(mb_ref, sb_ref, accb_ref, lb, pv_b)

    @pl.when(g == last)
    def _():
        kvn = kvn_ref[...]
        l_self = (jnp.sum(aq.astype(F32) * kvn[:, :LANES], axis=1, keepdims=True) + bself_ref[...]
                  + negself_ref[:, 0:1])
        oa = _flash_finish(ma_ref, sa_ref, acca_ref, l_self, kvn[:, LANES:])
        rowi = lax.broadcasted_iota(jnp.int32, oa.shape, 0)
        oa = jnp.where(rowi < H_A // H_A_KV, oa, pltpu.roll(oa, HD_A, 1))
        oa_ref[...] = oa[:, :HD_A]
        latn = latn_ref[...]
        lb_self = jnp.sum(qlat.astype(F32) * latn, axis=1, keepdims=True)
        ol_ref[...] = _flash_finish(mb_ref, sb_ref, accb_ref, lb_self, latn[:, :KV_LORA]).astype(BF16)


def _dec_even(page_table, neg, negself, aq_s, qlat_s, kv_new, lat_new, near_s, bself, cache_kv_t, cache_lat_t,
              layer):
    nb, npg = page_table.shape
    n = PAGES_PER_STEP if npg % PAGES_PER_STEP == 0 else 1
    page = cache_kv_t.shape[3]
    per_seq = lambda shape: pl.BlockSpec((None,) + shape, lambda b, g, pt: (b,) + (0,) * len(shape))
    const = lambda shape: pl.BlockSpec(shape, lambda b, g, pt: (0,) * len(shape))
    grid_spec = pltpu.PrefetchScalarGridSpec(
        num_scalar_prefetch=1, grid=(nb, npg // n),
        in_specs=[pl.BlockSpec((None, 1, n * page), lambda b, g, pt: (b, 0, g)), per_seq((1, LANES)),
                  per_seq((H_A, LANES)), per_seq((H_B, 256)), per_seq((1, 256)), per_seq((1, 256)),
                  const((H_A, LANES)), const((H_A, 1))]
        + _page_specs((256, page), layer, n) + _page_specs((KV_LORA + ROPE_B, page), layer, n),
        out_specs=[per_seq((H_A, HD_A)), per_seq((H_B, KV_LORA))],
        scratch_shapes=[pltpu.VMEM((H_A, 1), F32), pltpu.VMEM((H_A, 1), F32), pltpu.VMEM((H_A, LANES), F32),
                        pltpu.VMEM((H_B, 1), F32), pltpu.VMEM((H_B, 1), F32), pltpu.VMEM((H_B, KV_LORA), F32)])
    return pl.pallas_call(
        functools.partial(_dec_even_kernel, n=n), grid_spec=grid_spec,
        out_shape=[jax.ShapeDtypeStruct((nb, H_A, HD_A), F32), jax.ShapeDtypeStruct((nb, H_B, KV_LORA), BF16)],
        compiler_params=_params(("parallel", "arbitrary"), 32 * 2 ** 20),
        name="decode_attn_even",
    )(page_table, neg, negself, aq_s, qlat_s, kv_new, lat_new, near_s, bself,
      *([cache_kv_t] * n), *([cache_lat_t] * n))


def _dec_odd_kernel(pt_ref, lam_ref, q_ref, kvn_ref, near_ref, bself_ref, subg_ref, *refs, n, lam_init):
    pages = refs[:n]
    o_ref = refs[n]
    m_ref, s_ref, acc_ref = refs[n + 1:]
    g = pl.program_id(1)
    last = pl.num_programs(1) - 1
    page = pages[0].shape[0] // (2 * H_C_KV)
    width = 2 * HD_C

    @pl.when(g == 0)
    def _():
        m_ref[...] = jnp.full(m_ref.shape, NEG_INF, F32)
        s_ref[...] = jnp.zeros(s_ref.shape, F32)
        acc_ref[...] = jnp.zeros(acc_ref.shape, F32)

    q = q_ref[...]
    row_g0 = lax.broadcasted_iota(jnp.int32, (2 * H_C, width), 0) % H_C < H_C // H_C_KV
    part = lambda p, r: p[pl.ds(r, page, stride=2 * H_C_KV), :].astype(BF16)
    logits = jnp.concatenate(
        [_dot_nt(q[:, :width], part(p, 0)) + _dot_nt(q[:, width:], part(p, 1)) for p in pages], axis=1)
    logits = _add_near_last_page(logits, jnp.where(g == last, near_ref[...], 0.0), n)

    def pv(p):
        v0 = sum(_dot(p[:, j * LANES:(j + 1) * LANES], part(pages[j], 2)) for j in range(n))
        v1 = sum(_dot(p[:, j * LANES:(j + 1) * LANES], part(pages[j], 3)) for j in range(n))
        return jnp.where(row_g0, v0, v1)

    _flash_update(m_ref, s_ref, acc_ref, logits, pv)

    @pl.when(g == last)
    def _():
        kvn = kvn_ref[...]
        l_self = jnp.sum(q.astype(F32) * kvn[:, :2 * width], axis=1, keepdims=True) + bself_ref[...]
        v_self = jnp.where(row_g0, kvn[:, 2 * width:3 * width], kvn[:, 3 * width:])
        o = _flash_finish(m_ref, s_ref, acc_ref, l_self, v_self)
        d = o[:H_C] - lam_ref[0] * o[H_C:]
        o_ref[...] = _rms(d, subg_ref[...]) * (1.0 - lam_init)


def _dec_odd(page_table, lam, q_s, kv_new, near_s, bself, subg, cache_kv_r, layer, lam_init):
    nb, npg = page_table.shape
    n = PAGES_PER_STEP if npg % PAGES_PER_STEP == 0 else 1
    rows, width = cache_kv_r.shape[2:]
    per_seq = lambda shape: pl.BlockSpec((None,) + shape, lambda b, g, pt: (b,) + (0,) * len(shape))
    const = lambda shape: pl.BlockSpec(shape, lambda b, g, pt: (0,) * len(shape))
    grid_spec = pltpu.PrefetchScalarGridSpec(
        num_scalar_prefetch=1, grid=(nb, npg // n),
        in_specs=[pl.BlockSpec(memory_space=pltpu.SMEM), per_seq((2 * H_C, H_C_KV * width)),
                  per_seq((1, 2 * H_C_KV * width)), const((2 * H_C, LANES)), const((2 * H_C, 1)), const((1, width))]
        + _page_specs((rows, width), layer, n),
        out_specs=per_seq((H_C, width)),
        scratch_shapes=[pltpu.VMEM((2 * H_C, 1), F32), pltpu.VMEM((2 * H_C, 1), F32),
                        pltpu.VMEM((2 * H_C, width), F32)])
    return pl.pallas_call(
        functools.partial(_dec_odd_kernel, n=n, lam_init=lam_init), grid_spec=grid_spec,
        out_shape=jax.ShapeDtypeStruct((nb, H_C, width), F32),
        compiler_params=_params(("parallel", "arbitrary"), 32 * 2 ** 20),
        name="decode_attn_odd",
    )(page_table, lam, q_s, kv_new, near_s, bself, subg, *([cache_kv_r] * n))


def _t5_bias(table, rel):
    n = jnp.maximum(rel, 0)
    max_exact = NUM_BUCKETS // 2
    nf = jnp.maximum(n, 1).astype(F32)
    large = max_exact + (jnp.log(nf / max_exact) / math.log(MAX_DISTANCE / max_exact)
                         * (NUM_BUCKETS - max_exact)).astype(jnp.int32)
    bucket = jnp.where(n < max_exact, n, jnp.minimum(large, NUM_BUCKETS - 1))
    return table[bucket].astype(F32)


def _far_distance_is_constant(first_far, max_rel):
    n = np.arange(first_far, max_rel + 1, dtype=np.float64)
    large = NUM_BUCKETS // 2 + np.floor(np.log(n / (NUM_BUCKETS // 2)) / math.log(MAX_DISTANCE / (NUM_BUCKETS // 2))
                                        * (NUM_BUCKETS - NUM_BUCKETS // 2) - 1e-3)
    return bool(np.all(large >= NUM_BUCKETS - 1))


def _near_tiles(table):
    r = jnp.arange(Q_BLOCK, dtype=jnp.int32)[:, None]
    c = jnp.arange(2 * Q_BLOCK, dtype=jnp.int32)[None, :]
    near = _t5_bias(table, Q_BLOCK + r - c) - table[NUM_BUCKETS - 1].astype(F32)
    return jnp.transpose(near, (2, 0, 1))


def _near_rows(table, page):
    rel = page - jnp.arange(page, dtype=jnp.int32)
    far = table[NUM_BUCKETS - 1].astype(F32)
    return (_t5_bias(table, rel) - far).T, (_t5_bias(table, jnp.zeros((1,), jnp.int32)) - far).T


def _rope_tables(pos, rows):
    half = ROPE_B // 2
    inv = ROPE_THETA ** (-jnp.arange(half, dtype=F32) / half)
    ang = pos.astype(F32)[:, None] * inv[None, :]
    cos, sin = jnp.cos(ang), jnp.sin(ang)
    z = jnp.zeros_like(cos)
    pad = jnp.zeros((pos.shape[0], LANES - ROPE_B), F32)
    tabs = [jnp.concatenate([cos, cos, pad], 1), jnp.concatenate([sin, sin, pad], 1),
            jnp.concatenate([-sin, z, pad], 1), jnp.concatenate([z, sin, pad], 1)]
    return [jnp.broadcast_to(t, (rows, LANES)) if t.shape[0] != rows else t for t in tabs]


def _prep_ab_weights(w_in, w_uq, w_uk, w_uv):
    d = w_in.shape[0]
    o = np.cumsum([0, H_A * HD_A, H_A_KV * HD_A, H_A_KV * HD_A, H_IDX * HD_IDX, HD_IDX, H_IDX, Q_LORA, KV_LORA,
                   ROPE_B])
    sec = lambda i: w_in[:, o[i]:o[i + 1]]
    w1 = jnp.concatenate([sec(0), sec(1), sec(2), sec(3), sec(6), sec(7), sec(8), sec(5),
                          jnp.zeros((d, GRP_IK - GRP_IW - H_IDX), w_in.dtype), sec(4)], axis=1).astype(BF16)
    assert w1.shape[1] == AB_COLS
    uq = w_uq.reshape(Q_LORA, H_B, NOPE_B + ROPE_B)
    nope = uq[:, :, :NOPE_B].reshape(Q_LORA, H_B * NOPE_B)
    r = uq[:, :, NOPE_B:]
    half = ROPE_B // 2
    rot = jnp.concatenate([-r[:, :, half:], r[:, :, :half]], axis=-1)
    padl = lambda a: jnp.pad(a, ((0, 0), (0, 0), (0, LANES - ROPE_B))).reshape(Q_LORA, H_B * LANES)
    w2 = jnp.concatenate([nope, padl(r), padl(rot)], axis=1).astype(BF16)
    eye = jnp.eye(H_B, dtype=w_uk.dtype)
    wuk = jnp.einsum("chn,hg->hngc", w_uk, eye).reshape(H_B * NOPE_B, H_B * KV_LORA).astype(BF16)
    wv = jnp.transpose(w_uv, (1, 0, 2))
    zl = jnp.zeros_like(wv)
    even = (jnp.arange(H_B) % 2 == 0)[:, None, None]
    wuv = jnp.where(even, jnp.concatenate([wv, zl], -1), jnp.concatenate([zl, wv], -1)).astype(BF16)
    wuv_bd = jnp.einsum("chv,hg->hcgv", w_uv, eye).reshape(H_B * KV_LORA, H_B * V_B).astype(BF16)
    return w1, w2, wuk, wuv, wuv_bd


def _lambda(lambda_qk, layer):
    lam_init = 0.8 - 0.6 * math.exp(-0.3 * layer)
    lq = lambda_qk.astype(F32)
    lam = jnp.exp(jnp.sum(lq[0] * lq[1])) - jnp.exp(jnp.sum(lq[2] * lq[3])) + lam_init
    return lam.reshape(1), lam_init


def _run_prompt(x, P):
    nb, t, d = x.shape
    assert t % Q_BLOCK == 0
    m = nb * t
    topk = min(TOPK_MAX, t // 4)
    xf = x.reshape(m, d)
    tabs = _rope_tables(jnp.arange(t, dtype=jnp.int32), t)
    a_kv_l, a_idx_l, b_lat_l, c_kv_l = [], [], [], []
    depth = P["attn_norm"].shape[0]
    for l in range(depth):
        last = l == depth - 1
        if l % 2 == 0:
            e = l // 2
            w1, w2, wuk, wuv, _ = P["ab"][e]
            (aq, akv, iq, ik, iw, qlat, blat, av2, latc, akt, ikt, latt) = _ab_proj(
                xf, P["attn_norm"][l][None], w1, P["g_q"][e][None], w2, wuk, P["g_kv"][e][None], tabs, nb, t, True)
            o = _attn_even(aq, iq, iw, qlat, akt, av2, ikt, latt, latc, P["near_a"], wuv, nb, t, topk)
            wo = P["w_o_ab"][e]
            a_kv_l.append(akv.reshape(nb, t, 2, H_A_KV, HD_A))
            a_idx_l.append(ik.reshape(nb, t, HD_IDX))
            b_lat_l.append(blat.reshape(nb, t, KV_LORA + ROPE_B))
        else:
            j = l // 2
            lam, lam_init = _lambda(P["lambda_qk"][j], l)
            q, ckv, ckt, cv = _c_proj(xf, P["attn_norm"][l][None], P["w_in_c"][j], nb, t, True)
            o = _attn_odd(lam, q, ckt, cv, P["near_c"], P["sub_norm_c"][j][None], nb, t, lam_init)
            wo = P["w_o_c"][j]
            c_kv_l.append(ckv.reshape(nb, t, 2, H_C_KV, 2 * HD_C))
        xf = _out_ffn(xf, o, wo, P["ffn_norm"][l][None], P["w_gate_up"][l], P["w_down"][l], P["final_norm"][None],
                      last)
    return (xf.reshape(nb, t, d), jnp.stack(a_kv_l), jnp.stack(a_idx_l), jnp.stack(b_lat_l), jnp.stack(c_kv_l))


def _run_sample(x, P, caches, page_table):
    cache_a_kv, cache_a_idx, cache_b_latent, cache_c_kv = caches
    nb, t, d = x.shape
    assert t == 1
    npg = page_table.shape[1]
    page = cache_a_idx.shape[2]
    past = npg * page
    topk = min(TOPK_MAX, (past + t) // 4)
    assert _far_distance_is_constant(page + 1, past)
    xf = x.reshape(nb, d)
    tabs = _rope_tables(jnp.full((1,), past, jnp.int32), nb)
    idx_t = jnp.transpose(cache_a_idx, (0, 1, 3, 2))
    kv_t = jnp.transpose(cache_a_kv, (0, 1, 3, 4, 5, 2)).reshape(cache_a_kv.shape[:2] + (-1, page))
    lat_t = jnp.transpose(cache_b_latent, (0, 1, 3, 2))
    ckv_r = cache_c_kv.reshape(cache_c_kv.shape[:2] + (-1, cache_c_kv.shape[-1]))
    a_kv_l, a_idx_l, b_lat_l, c_kv_l = [], [], [], []
    depth = P["attn_norm"].shape[0]
    gsz_a = H_A // H_A_KV
    head_lo_a = (jnp.arange(H_A) < gsz_a)[None, :, None]
    for l in range(depth):
        last = l == depth - 1
        if l % 2 == 0:
            e = l // 2
            w1, w2, wuk, _, wuv_bd = P["ab"][e]
            aq, akv, iq, ik, iw, qlat, blat = _ab_proj(
                xf, P["attn_norm"][l][None], w1, P["g_q"][e][None], w2, wuk, P["g_kv"][e][None], tabs, nb, t, False)
            iq_s = jnp.pad(iq.reshape(nb, H_IDX, HD_IDX), ((0, 0), (0, 8 - H_IDX), (0, 0)))
            iw_s = jnp.pad(iw, ((0, 0), (0, 8 - H_IDX)))[:, :, None]
            scores = _dec_scores(page_table, iq_s, iw_s, idx_t, e)
            neg, negself = _dec_select(scores.reshape(nb, past), iq, ik, iw, topk)
            aq3 = aq.reshape(nb, H_A, HD_A)
            zq = jnp.zeros_like(aq3)
            aq_s = jnp.where(head_lo_a, jnp.concatenate([aq3, zq], -1), jnp.concatenate([zq, aq3], -1))
            lat_new = jnp.pad(blat, ((0, 0), (0, 256 - blat.shape[1])))
            oa, olat = _dec_even(page_table, neg.reshape(nb, 1, past), negself.reshape(nb, 1, LANES), aq_s,
                                 qlat.reshape(nb, H_B, 256), akv.reshape(nb, 1, 256), lat_new.reshape(nb, 1, 256),
                                 P["near_a_s"], P["bself_a"], kv_t, lat_t, e)
            ob = _mm(olat.reshape(nb, H_B * KV_LORA), wuv_bd, BF16)
            o = jnp.concatenate([oa.reshape(nb, H_A * HD_A).astype(BF16), ob], axis=1)
            wo = P["w_o_ab"][e]
            a_kv_l.append(akv.reshape(nb, t, 2, H_A_KV, HD_A))
            a_idx_l.append(ik.reshape(nb, t, HD_IDX))
            b_lat_l.append(blat.reshape(nb, t, KV_LORA + ROPE_B))
        else:
            j = l // 2
            lam, lam_init = _lambda(P["lambda_qk"][j], l)
            q, ckv = _c_proj(xf, P["attn_norm"][l][None], P["w_in_c"][j], nb, t, False)
            q4 = q.reshape(nb, H_C, 2, HD_C)
            sel = ((jnp.arange(H_C)[:, None, None, None] // (H_C // H_C_KV) == jnp.arange(H_C_KV)[None, None, :, None])
                   & (jnp.arange(2)[None, :, None, None] == jnp.arange(2)[None, None, None, :]))
            q_s = jnp.where(sel[None, :, :, :, :, None], q4[:, :, :, None, None, :], jnp.zeros((), q.dtype))
            q_s = jnp.transpose(q_s, (0, 2, 1, 3, 4, 5)).reshape(nb, 2 * H_C, H_C_KV * 2 * HD_C)
            oc = _dec_odd(page_table, lam, q_s, ckv.reshape(nb, 1, -1), P["near_c_s"], P["bself_c"],
                          P["sub_norm_c"][j][None], ckv_r, j, lam_init)
            o = oc.reshape(nb, H_C * 2 * HD_C).astype(BF16)
            wo = P["w_o_c"][j]
            c_kv_l.append(ckv.reshape(nb, t, 2, H_C_KV, 2 * HD_C))
        xf = _out_ffn(xf, o, wo, P["ffn_norm"][l][None], P["w_gate_up"][l], P["w_down"][l], P["final_norm"][None],
                      last)
    return (xf.reshape(nb, t, d), jnp.stack(a_kv_l), jnp.stack(a_idx_l), jnp.stack(b_lat_l), jnp.stack(c_kv_l))


def kernel(x_prompt, x_sample, cache_a_kv, cache_a_idx, cache_b_latent, cache_c_kv, page_table, attn_norm, w_in_ab,
           w_uq, g_q, g_kv, w_uk, w_uv, w_o_ab, rel_bias, w_in_c, lambda_qk, sub_norm_c, w_o_c, ffn_norm, w_gate_up,
           w_down, final_norm):
    page = cache_a_idx.shape[2]
    assert page == LANES
    assert _far_distance_is_constant(Q_BLOCK + 1, max(x_prompt.shape[1], Q_BLOCK + 1))
    tab_a, tab_c = rel_bias[:, :H_A], rel_bias[:, H_A:]
    near_a_s, bself_a = _near_rows(tab_a, page)
    near_c_s, bself_c = _near_rows(tab_c, page)
    P = dict(
        attn_norm=attn_norm, g_q=g_q, g_kv=g_kv, ffn_norm=ffn_norm, final_norm=final_norm, lambda_qk=lambda_qk,
        sub_norm_c=sub_norm_c,
        ab=[_prep_ab_weights(w_in_ab[e], w_uq[e], w_uk[e], w_uv[e]) for e in range(w_in_ab.shape[0])],
        w_o_ab=w_o_ab.astype(BF16), w_in_c=w_in_c.astype(BF16), w_o_c=w_o_c.astype(BF16),
        w_gate_up=w_gate_up.astype(BF16), w_down=w_down.astype(BF16),
        near_a=_near_tiles(tab_a), near_c=_near_tiles(tab_c),
        near_a_s=near_a_s, bself_a=bself_a,
        near_c_s=jnp.concatenate([near_c_s, near_c_s], 0), bself_c=jnp.concatenate([bself_c, bself_c], 0),
    )
    y_p, akv_p, aidx_p, blat_p, ckv_p = _run_prompt(x_prompt, P)
    y_s, akv_s, aidx_s, blat_s, ckv_s = _run_sample(
        x_sample, P, (cache_a_kv, cache_a_idx, cache_b_latent, cache_c_kv), page_table)
    return (y_p, y_s, akv_p, aidx_p, blat_p, ckv_p, akv_s, aidx_s, blat_s, ckv_s)
```

```python
import functools
import math

import numpy as np
import jax
import jax.numpy as jnp
from jax import lax
from jax.experimental import pallas as pl
from jax.experimental.pallas import tpu as pltpu

EPS = 1e-6
NUM_BUCKETS = 32
MAX_DISTANCE = 128
H_A, HD_A, H_A_KV = 8, 64, 2
H_IDX, HD_IDX = 4, 64
TOPK_MAX = 256
H_B, NOPE_B, ROPE_B, V_B = 8, 64, 32, 64
Q_LORA, KV_LORA = 256, 128
ROPE_THETA = 10000.0
MLA_SCALE = (NOPE_B + ROPE_B) ** -0.5
H_C, HD_C, H_C_KV = 8, 64, 2

LANES = 128
V7X_VMEM_BYTES = 64 * 2 ** 20

Q_BLOCK = LANES
F32 = jnp.float32
BF16 = jnp.bfloat16
NEG_INF = float("-inf")
SUM_ROWS = 64
LOG2E = 1.4426950408889634
INT_MIN = -2 ** 31


def _vmem_limit(nbytes):
    return int(min(max(nbytes, 32 * 2 ** 20), V7X_VMEM_BYTES - 8 * 2 ** 20))


def _params(semantics, vmem_bytes):
    return pltpu.CompilerParams(dimension_semantics=semantics, vmem_limit_bytes=_vmem_limit(vmem_bytes))


def _rms(x, g):
    return x * lax.rsqrt(jnp.mean(x * x, axis=-1, keepdims=True) + EPS) * g


def _dot(a, b):
    return jnp.dot(a, b, preferred_element_type=F32)


def _dot_nt(a, b):
    return lax.dot_general(a, b, (((1,), (1,)), ((), ())), preferred_element_type=F32)


def _const_spec(shape):
    nd = len(shape)
    return pl.BlockSpec(shape, lambda *_: (0,) * nd, pipeline_mode=pl.Buffered(1))


def _ordered_bits_to_float(u):
    key = jnp.maximum(u ^ jnp.int32(INT_MIN), jnp.int32(-0x7F800001))
    bits = jnp.where(key >= 0, key, key ^ jnp.int32(0x7FFFFFFF))
    return lax.bitcast_convert_type(bits, F32)


def _topk_neg_mask(score, idx, k, axis):
    n = score.shape[axis]
    state = (score.shape[0], 1) if axis == 1 else (1, score.shape[1])
    kf = jnp.float32(k)

    def reduce_sum(x):
        if axis == 0 and n % SUM_ROWS == 0 and n > SUM_ROWS:
            x = jnp.sum(x.reshape(n // SUM_ROWS, SUM_ROWS, x.shape[1]), axis=0)
        return jnp.sum(x, axis=axis, keepdims=True)

    def count(mask):
        return reduce_sum(jnp.where(mask, 1.0, 0.0))

    def value_step(it, u):
        c = u | lax.shift_left(jnp.int32(1), 31 - it)
        return jnp.where(count(score >= _ordered_bits_to_float(c)) >= kf, c, u)

    u = lax.fori_loop(0, 32, value_step, jnp.zeros(state, jnp.int32))
    thr = _ordered_bits_to_float(u)
    gt = score > thr
    eq = score == thr
    need = kf - count(gt)
    n_eq = count(eq)
    nbits = int(n).bit_length()

    def tie_search():
        eqf = jnp.where(eq, 1.0, 0.0)

        def idx_step(it, j):
            c = j | lax.shift_left(jnp.int32(1), nbits - 1 - it)
            m = reduce_sum(jnp.where(idx < c, eqf, 0.0))
            return jnp.where(m <= need, c, j)

        return lax.fori_loop(0, nbits, idx_step, jnp.zeros(state, jnp.int32))

    overflow = jnp.max(n_eq - need) > 0.0
    jcut = lax.cond(overflow, tie_search, lambda: jnp.full(state, 2 ** nbits - 1, jnp.int32))
    keep_tie = jnp.where(idx < jcut, 0.0, NEG_INF)
    return jnp.where(gt, 0.0, jnp.where(eq, keep_tie, NEG_INF))


AB_AQ, AB_AK, AB_AV, AB_IQ, AB_CQ, AB_CKV, AB_GRP, AB_COLS = 0, 512, 640, 768, 1024, 1280, 1408, 1536
GRP_KR, GRP_IW, GRP_IK = 0, 32, 64


def _ab_proj_kernel(x_ref, g_ref, w1_ref, gq_ref, w2_ref, wuk_ref, gkv_ref, cos_ref, sin_ref, nsl_ref, shi_ref,
                    aq_ref, akv_ref, iq_ref, ik_ref, iw_ref, qlat_ref, blat_ref, *t_refs, emit_t):
    h = _rms(x_ref[...], g_ref[...]).astype(BF16)
    y = _dot(h, w1_ref[...])
    aq_ref[...] = (y[:, AB_AQ:AB_AK] * (HD_A ** -0.5 * LOG2E)).astype(BF16)
    akv_ref[...] = y[:, AB_AK:AB_IQ]
    iq_ref[...] = (y[:, AB_IQ:AB_CQ] * HD_IDX ** -0.5).astype(BF16)
    grp = y[:, AB_GRP:AB_COLS]
    ik_ref[...] = grp[:, GRP_IK:GRP_IK + HD_IDX]
    iw_ref[...] = grp[:, GRP_IW:GRP_IW + H_IDX] * H_IDX ** -0.5
    cq = _rms(y[:, AB_CQ:AB_CKV], gq_ref[...]).astype(BF16)
    q2 = _dot(cq, w2_ref[...])
    n_nope = H_B * NOPE_B
    n_pad = H_B * LANES
    qc = _dot(q2[:, :n_nope].astype(BF16), wuk_ref[...])
    cos8 = jnp.concatenate([cos_ref[...]] * H_B, axis=1)
    sin8 = jnp.concatenate([sin_ref[...]] * H_B, axis=1)
    qr = q2[:, n_nope:n_nope + n_pad] * cos8 + q2[:, n_nope + n_pad:] * sin8
    pieces = []
    for hh in range(H_B):
        pieces += [qc[:, hh * LANES:(hh + 1) * LANES], qr[:, hh * LANES:(hh + 1) * LANES]]
    qlat_ref[...] = (jnp.concatenate(pieces, axis=1) * (MLA_SCALE * LOG2E)).astype(BF16)
    latc = _rms(y[:, AB_CKV:AB_GRP], gkv_ref[...])
    half = ROPE_B // 2
    kr = (grp * cos_ref[...] + pltpu.roll(grp, LANES - half, 1) * nsl_ref[...]
          + pltpu.roll(grp, half, 1) * shi_ref[...])
    blat_ref[:, :KV_LORA] = latc
    blat_ref[:, KV_LORA:] = kr[:, :ROPE_B]
    if emit_t:
        av2_ref, latc_ref, ik2_ref, akt_ref, iwt_ref, latt_ref = t_refs
        lane = lax.broadcasted_iota(jnp.int32, (y.shape[0], LANES), 1)
        v = y[:, AB_AV:AB_IQ]
        vs = pltpu.roll(v, HD_A, 1)
        av2_ref[...] = jnp.concatenate([jnp.where(lane < HD_A, v, vs), jnp.where(lane < HD_A, vs, v)],
                                       axis=1).astype(BF16)
        latc_ref[...] = latc.astype(BF16)
        kt = y[:, AB_AK:AB_AV].T
        akt_ref[...] = jnp.concatenate([kt[:HD_A], kt[:HD_A], kt[HD_A:], kt[HD_A:]], axis=0).astype(BF16)
        ik2_ref[...] = jnp.where(lane < HD_IDX, pltpu.roll(grp, HD_IDX, 1), grp).astype(BF16)
        iwt_ref[...] = grp.T[GRP_IW:GRP_IW + 8] * H_IDX ** -0.5
        latt_ref[...] = jnp.concatenate([latc.T, kr.T], axis=0).astype(BF16)


def _ab_proj(x, g, w1, gq, w2, wuk, gkv, rope_tabs, n_batch, t, emit_t):
    m = x.shape[0]
    tm = min(512, t) if emit_t else m
    assert m % tm == 0 and (not emit_t or t % tm == 0)
    tpb = max(t // tm, 1)
    row = lambda cols: pl.BlockSpec((tm, cols), lambda i: (i, 0))
    tab = pl.BlockSpec((tm, LANES), (lambda i: (i % tpb, 0)) if emit_t else (lambda i: (0, 0)))
    tsp = lambda rows: pl.BlockSpec((None, rows, tm), lambda i: (i // tpb, 0, i % tpb))
    out_shape = [jax.ShapeDtypeStruct((m, 512), BF16), jax.ShapeDtypeStruct((m, 256), F32),
                 jax.ShapeDtypeStruct((m, 256), BF16), jax.ShapeDtypeStruct((m, HD_IDX), F32),
                 jax.ShapeDtypeStruct((m, H_IDX), F32), jax.ShapeDtypeStruct((m, H_B * 256), BF16),
                 jax.ShapeDtypeStruct((m, KV_LORA + ROPE_B), F32)]
    out_specs = [row(512), row(256), row(256), row(HD_IDX), row(H_IDX), row(H_B * 256), row(KV_LORA + ROPE_B)]
    if emit_t:
        out_shape += [jax.ShapeDtypeStruct((m, 256), BF16), jax.ShapeDtypeStruct((m, KV_LORA), BF16),
                      jax.ShapeDtypeStruct((m, 2 * HD_IDX), BF16),
                      jax.ShapeDtypeStruct((n_batch, 256, t), BF16), jax.ShapeDtypeStruct((n_batch, 8, t), F32),
                      jax.ShapeDtypeStruct((n_batch, 256, t), BF16)]
        out_specs += [row(256), row(KV_LORA), row(2 * HD_IDX), tsp(256), tsp(8), tsp(256)]
    return pl.pallas_call(
        functools.partial(_ab_proj_kernel, emit_t=emit_t),
        grid=(m // tm,),
        in_specs=[row(x.shape[1]), _const_spec(g.shape), _const_spec(w1.shape), _const_spec(gq.shape),
                  _const_spec(w2.shape), _const_spec(wuk.shape), _const_spec(gkv.shape), tab, tab, tab, tab],
        out_specs=out_specs, out_shape=out_shape,
        compiler_params=_params(("parallel",), 40 * 2 ** 20),
        name="ab_proj_prompt" if emit_t else "ab_proj_sample",
    )(x, g, w1, gq, w2, wuk, gkv, *rope_tabs)


def _c_proj_kernel(x_ref, g_ref, w_ref, q_ref, ckv_ref, *t_refs, emit_t):
    h = _rms(x_ref[...], g_ref[...]).astype(BF16)
    y = _dot(h, w_ref[...])
    nq = H_C * 2 * HD_C
    nk = H_C_KV * 2 * HD_C
    q_ref[...] = (y[:, :nq] * (HD_C ** -0.5 * LOG2E)).astype(BF16)
    ckv_ref[...] = y[:, nq:]
    if emit_t:
        ckt_ref, cv_ref = t_refs
        ckt_ref[...] = y[:, nq:nq + nk].T.astype(BF16)
        cv_ref[...] = y[:, nq + nk:].astype(BF16)


def _c_proj(x, g, w, n_batch, t, emit_t):
    m = x.shape[0]
    tm = min(512, t) if emit_t else m
    tpb = max(t // tm, 1)
    nq, nk = H_C * 2 * HD_C, H_C_KV * 2 * HD_C
    row = lambda cols: pl.BlockSpec((tm, cols), lambda i: (i, 0))
    out_shape = [jax.ShapeDtypeStruct((m, nq), BF16), jax.ShapeDtypeStruct((m, 2 * nk), F32)]
    out_specs = [row(nq), row(2 * nk)]
    if emit_t:
        out_shape += [jax.ShapeDtypeStruct((n_batch, nk, t), BF16), jax.ShapeDtypeStruct((m, nk), BF16)]
        out_specs += [pl.BlockSpec((None, nk, tm), lambda i: (i // tpb, 0, i % tpb)), row(nk)]
    return pl.pallas_call(
        functools.partial(_c_proj_kernel, emit_t=emit_t),
        grid=(m // tm,),
        in_specs=[row(x.shape[1]), _const_spec(g.shape), _const_spec(w.shape)],
        out_specs=out_specs, out_shape=out_shape,
        compiler_params=_params(("parallel",), 32 * 2 ** 20),
        name="c_proj_prompt" if emit_t else "c_proj_sample",
    )(x, g, w)


def _out_ffn_kernel(x_ref, o_ref, wo_ref, g_ref, wgu_ref, wd_ref, fg_ref, out_ref, *, d_ff, chunk, final):
    x1 = x_ref[...] + _dot(o_ref[...], wo_ref[...])
    h = _rms(x1, g_ref[...]).astype(BF16)
    acc = x1
    for c in range(d_ff // chunk):
        gate = _dot(h, wgu_ref[:, c * chunk:(c + 1) * chunk])
        up = _dot(h, wgu_ref[:, d_ff + c * chunk:d_ff + (c + 1) * chunk])
        act = gate * jax.nn.sigmoid(gate) * up
        acc = acc + _dot(act.astype(BF16), wd_ref[c * chunk:(c + 1) * chunk, :])
    if final:
        acc = _rms(acc, fg_ref[...])
    out_ref[...] = acc


def _out_ffn(x, o, wo, g, wgu, wd, fg, final):
    m, d = x.shape
    d_ff = wd.shape[0]
    chunk = 256 if d_ff % 256 == 0 else LANES
    assert d_ff % chunk == 0
    tm = min(512, m)
    assert m % tm == 0
    row = lambda cols: pl.BlockSpec((tm, cols), lambda i: (i, 0))
    return pl.pallas_call(
        functools.partial(_out_ffn_kernel, d_ff=d_ff, chunk=chunk, final=final),
        grid=(m // tm,),
        in_specs=[row(d), row(o.shape[1]), _const_spec(wo.shape), _const_spec(g.shape), _const_spec(wgu.shape),
                  _const_spec(wd.shape), _const_spec(fg.shape)],
        out_specs=row(d), out_shape=jax.ShapeDtypeStruct((m, d), F32),
        compiler_params=_params(("parallel",), 52 * 2 ** 20),
        name="out_ffn",
    )(x, o, wo, g, wgu, wd, fg)


def _mm_kernel(a_ref, b_ref, o_ref):
    o_ref[...] = _dot(a_ref[...], b_ref[...]).astype(o_ref.dtype)


def _mm(a, b, out_dtype):
    return pl.pallas_call(_mm_kernel, out_shape=jax.ShapeDtypeStruct((a.shape[0], b.shape[1]), out_dtype),
                          name="small_matmul")(a, b)


def _softmax_parts(logits):
    m = jnp.max(logits, axis=1, keepdims=True)
    p = jnp.exp2(logits - m)
    return p.astype(BF16), 1.0 / jnp.sum(p, axis=1, keepdims=True)


def _add_to_blocks(logits, addends):
    nb = logits.shape[1] // Q_BLOCK
    cols = [logits[:, j * Q_BLOCK:(j + 1) * Q_BLOCK] for j in range(nb)]
    return jnp.concatenate([cols[j] + addends[j] if j in addends else cols[j] for j in range(nb)], axis=1)


def _pipelined(units):
    pending = units[0][0]()
    for i, (_, finish) in enumerate(units):
        nxt = units[i + 1][0]() if i + 1 < len(units) else None
        finish(pending)
        pending = nxt


def _near_addends(near_ref, head, r, first, per):
    return {first - 1 + jj: near_ref[head, per - 1 - r + jj] for jj in range(per + 1) if first - 1 + jj >= 0}


def _diag_causal(qb, first, per):
    w = per * Q_BLOCK
    row = lax.broadcasted_iota(jnp.int32, (Q_BLOCK, w), 0)
    col = lax.broadcasted_iota(jnp.int32, (Q_BLOCK, w), 1)
    return jnp.where(first * Q_BLOCK + col <= qb * Q_BLOCK + row, 0.0, NEG_INF)


def _attn_even_body(cls, per, qb, topk, aq_ref, iq_ref, iwt_ref, qlat_ref, akt_ref, av2_ref, ik2_ref, latt_ref,
                    latc_ref, near_ref, wuv_ref, o_ref):
    first = cls * per
    s = (cls + 1) * per * Q_BLOCK
    r = qb - first
    lo = lax.broadcasted_iota(jnp.int32, (Q_BLOCK, LANES), 1) < HD_A
    zero = jnp.zeros((), BF16)
    key = lax.broadcasted_iota(jnp.int32, (s, Q_BLOCK), 0)
    adm_t = key <= qb * Q_BLOCK + lax.broadcasted_iota(jnp.int32, (s, Q_BLOCK), 1)
    iq = iq_ref[...]
    iwt = iwt_ref[...]
    ik2 = ik2_ref[:s, :]
    score_t = jnp.zeros((s, Q_BLOCK), F32)
    for hp in range(H_IDX // 2):
        grp = iq[:, hp * LANES:(hp + 1) * LANES]
        q2 = jnp.concatenate([jnp.where(lo, grp, zero), jnp.where(lo, zero, grp)], axis=0)
        sc = jnp.maximum(_dot_nt(ik2, q2), 0.0)
        score_t = (score_t + sc[:, :Q_BLOCK] * iwt[2 * hp:2 * hp + 1, :]
                   + sc[:, Q_BLOCK:] * iwt[2 * hp + 1:2 * hp + 2, :])
    score_t = jnp.where(adm_t, score_t, NEG_INF)
    neg_sel = jnp.where(adm_t, _topk_neg_mask(score_t, key, topk, 0), NEG_INF).T
    aq = aq_ref[...]
    gsz = H_A // H_A_KV
    diag = _diag_causal(qb, first, per)
    causal = {first + j: diag[:, j * Q_BLOCK:(j + 1) * Q_BLOCK] for j in range(per)}
    base = H_A * HD_A
    held = {}

    def issue_a(head):
        g, pair, e = head // gsz, (head % gsz) // 2, head % 2
        grp = aq[:, (g * gsz // 2 + pair) * LANES:(g * gsz // 2 + pair + 1) * LANES]
        qm = jnp.where(lo, grp, zero) if e == 0 else jnp.where(lo, zero, grp)
        logits = _dot(qm, akt_ref[g * LANES:(g + 1) * LANES, :s]) + neg_sel
        return _add_to_blocks(logits, _near_addends(near_ref, head, r, first, per))

    def finish_a(head, logits):
        g = head // gsz
        p, inv = _softmax_parts(logits)
        o = _dot(p, av2_ref[:s, g * LANES:(g + 1) * LANES]) * inv
        if head % 2 == 0:
            held["a"] = o
        else:
            o_ref[:, (head // 2) * LANES:(head // 2 + 1) * LANES] = jnp.where(lo, held.pop("a"), o).astype(BF16)

    def issue_b(head):
        return _add_to_blocks(_dot(qlat_ref[:, head * 256:(head + 1) * 256], latt_ref[:, :s]), causal)

    def finish_b(head, logits):
        p, inv = _softmax_parts(logits)
        olat = (_dot(p, latc_ref[:s, :]) * inv).astype(BF16)
        part = _dot(olat, wuv_ref[head])
        if head % 2 == 0:
            held["b"] = part
        else:
            col0 = base + (head // 2) * LANES
            o_ref[:, col0:col0 + LANES] = (held.pop("b") + part).astype(BF16)

    _pipelined([(functools.partial(issue_a, h), functools.partial(finish_a, h)) for h in range(H_A)]
               + [(functools.partial(issue_b, h), functools.partial(finish_b, h)) for h in range(H_B)])


def _attn_even_kernel(aq_ref, iq_ref, iwt_ref, qlat_ref, akt_ref, av2_ref, ik2_ref, latt_ref, latc_ref, near_ref,
                      wuv_ref, o_ref, *, t, n_cls, topk):
    qb = pl.program_id(1)
    per = (t // Q_BLOCK) // n_cls
    for c in range(n_cls):
        @pl.when(qb // per == c)
        def _(c=c):
            _attn_even_body(c, per, qb, topk, aq_ref, iq_ref, iwt_ref, qlat_ref, akt_ref, av2_ref, ik2_ref,
                            latt_ref, latc_ref, near_ref, wuv_ref, o_ref)


def _key_classes(t):
    nqb = t // Q_BLOCK
    return 8 if nqb % 8 == 0 else (4 if nqb % 4 == 0 else 1)


def _attn_even(aq, iq, iwt, qlat, akt, av2, ik2, latt, latc, near, wuv, n_batch, t, topk):
    nqb = t // Q_BLOCK
    qrow = lambda cols: pl.BlockSpec((Q_BLOCK, cols), lambda b, i: (b * nqb + i, 0))
    krow = lambda cols: pl.BlockSpec((t, cols), lambda b, i: (b, 0))
    kcol = lambda rows: pl.BlockSpec((None, rows, t), lambda b, i: (b, 0, 0))
    return pl.pallas_call(
        functools.partial(_attn_even_kernel, t=t, n_cls=_key_classes(t), topk=topk),
        grid=(n_batch, nqb),
        in_specs=[qrow(512), qrow(256), pl.BlockSpec((None, 8, Q_BLOCK), lambda b, i: (b, 0, i)), qrow(H_B * 256),
                  kcol(256), krow(256), krow(2 * HD_IDX), kcol(256), krow(KV_LORA), _const_spec(near.shape),
                  _const_spec(wuv.shape)],
        out_specs=qrow(1024), out_shape=jax.ShapeDtypeStruct((n_batch * t, 1024), BF16),
        compiler_params=_params(("parallel", "arbitrary"), 48 * 2 ** 20),
        name="attn_even_prompt",
    )(aq, iq, iwt, qlat, akt, av2, ik2, latt, latc, near, wuv)


def _attn_odd_body(cls, per, qb, lam, lam_init, q_ref, ckt_ref, cv_ref, near_ref, subg_ref, o_ref):
    first = cls * per
    s = (cls + 1) * per * Q_BLOCK
    r = qb - first
    diag = _diag_causal(qb, first, per)
    lo = lax.broadcasted_iota(jnp.int32, (Q_BLOCK, LANES), 1) < HD_C
    zero = jnp.zeros((), BF16)
    gsz = H_C // H_C_KV

    def qk(head, e):
        g = head // gsz
        grp = q_ref[:, head * LANES:(head + 1) * LANES]
        qm = jnp.where(lo, grp, zero) if e == 0 else jnp.where(lo, zero, grp)
        addends = _near_addends(near_ref, head, r, first, per)
        for j in range(per):
            addends[first + j] = addends[first + j] + diag[:, j * Q_BLOCK:(j + 1) * Q_BLOCK]
        return _add_to_blocks(_dot(qm, ckt_ref[g * LANES:(g + 1) * LANES, :s]), addends)

    def pv(head, logits):
        g = head // gsz
        p, inv = _softmax_parts(logits)
        return _dot(p, cv_ref[:s, g * LANES:(g + 1) * LANES]) * inv

    held = {}

    def finish(head, e, logits):
        o = pv(head, logits)
        if e == 0:
            held["o"] = o
        else:
            o = held.pop("o") - lam * o
            o = _rms(o, subg_ref[...]) * (1.0 - lam_init)
            o_ref[:, head * LANES:(head + 1) * LANES] = o.astype(BF16)

    _pipelined([(functools.partial(qk, h, e), functools.partial(finish, h, e)) for h in range(H_C) for e in range(2)])


def _attn_odd_kernel(lam_ref, q_ref, ckt_ref, cv_ref, near_ref, subg_ref, o_ref, *, t, n_cls, lam_init):
    qb = pl.program_id(1)
    per = (t // Q_BLOCK) // n_cls
    lam = lam_ref[0]
    for c in range(n_cls):
        @pl.when(qb // per == c)
        def _(c=c):
            _attn_odd_body(c, per, qb, lam, lam_init, q_ref, ckt_ref, cv_ref, near_ref, subg_ref, o_ref)


def _attn_odd(lam, q, ckt, cv, near, subg, n_batch, t, lam_init):
    nqb = t // Q_BLOCK
    qrow = lambda cols: pl.BlockSpec((Q_BLOCK, cols), lambda b, i: (b * nqb + i, 0))
    return pl.pallas_call(
        functools.partial(_attn_odd_kernel, t=t, n_cls=_key_classes(t), lam_init=lam_init),
        grid=(n_batch, nqb),
        in_specs=[pl.BlockSpec(memory_space=pltpu.SMEM), qrow(1024),
                  pl.BlockSpec((None, 256, t), lambda b, i: (b, 0, 0)), pl.BlockSpec((t, 256), lambda b, i: (b, 0)),
                  _const_spec(near.shape), _const_spec(subg.shape)],
        out_specs=qrow(1024), out_shape=jax.ShapeDtypeStruct((n_batch * t, 1024), BF16),
        compiler_params=_params(("parallel", "arbitrary"), 40 * 2 ** 20),
        name="attn_odd_prompt",
    )(lam, q, ckt, cv, near, subg)


PAGES_PER_STEP = 32


def _page_specs(block, layer, n):
    nd = len(block)

    def spec(j):
        return pl.BlockSpec((None, None) + block,
                            lambda b, g, pt: (layer, pt[b, g * n + j]) + (0,) * nd)

    return [spec(j) for j in range(n)]


def _dec_scores_kernel(pt_ref, iq_ref, iw_ref, *refs, n):
    pages, out_ref = refs[:n], refs[n]
    iq = iq_ref[...]
    iw = iw_ref[...]
    for j in range(n):
        sc = jnp.maximum(_dot(iq, pages[j][...].astype(BF16)), 0.0) * iw
        out_ref[:, j * LANES:(j + 1) * LANES] = jnp.sum(sc, axis=0, keepdims=True)


def _dec_scores(page_table, iq_s, iw_s, cache_idx_t, layer):
    nb, npg = page_table.shape
    n = PAGES_PER_STEP if npg % PAGES_PER_STEP == 0 else 1
    page = cache_idx_t.shape[3]
    assert page == LANES
    grid_spec = pltpu.PrefetchScalarGridSpec(
        num_scalar_prefetch=1, grid=(nb, npg // n),
        in_specs=[pl.BlockSpec((None, 8, HD_IDX), lambda b, g, pt: (b, 0, 0)),
                  pl.BlockSpec((None, 8, 1), lambda b, g, pt: (b, 0, 0))]
        + _page_specs((HD_IDX, page), layer, n),
        out_specs=pl.BlockSpec((None, 1, n * page), lambda b, g, pt: (b, 0, g)))
    return pl.pallas_call(
        functools.partial(_dec_scores_kernel, n=n), grid_spec=grid_spec,
        out_shape=jax.ShapeDtypeStruct((nb, 1, npg * page), F32),
        compiler_params=_params(("parallel", "arbitrary"), 32 * 2 ** 20),
        name="decode_index_scores",
    )(page_table, iq_s, iw_s, *([cache_idx_t] * n))


def _dec_select_kernel(sc_ref, iq_ref, ik_ref, iw_ref, neg_ref, negself_ref, *, topk):
    rows, past = sc_ref.shape
    iq = iq_ref[...].astype(F32)
    ik = ik_ref[...]
    iw = iw_ref[...]
    self_score = jnp.zeros((rows, 1), F32)
    for hh in range(H_IDX):
        s = jnp.sum(iq[:, hh * HD_IDX:(hh + 1) * HD_IDX] * ik, axis=1, keepdims=True)
        self_score = self_score + jnp.maximum(s, 0.0) * iw[:, hh:hh + 1]
    lane = lax.broadcasted_iota(jnp.int32, (rows, LANES), 1)
    tail = jnp.where(lane == 0, self_score, NEG_INF)
    score = jnp.concatenate([sc_ref[...], tail], axis=1)
    col = lax.broadcasted_iota(jnp.int32, score.shape, 1)
    neg = jnp.where(col <= past, _topk_neg_mask(score, col, topk, 1), NEG_INF)
    neg_ref[...] = neg[:, :past]
    negself_ref[...] = neg[:, past:]


def _dec_select(scores, iq, ik, iw, topk):
    rows, past = scores.shape
    return pl.pallas_call(
        functools.partial(_dec_select_kernel, topk=topk),
        out_shape=[jax.ShapeDtypeStruct((rows, past), F32), jax.ShapeDtypeStruct((rows, LANES), F32)],
        compiler_params=pltpu.CompilerParams(vmem_limit_bytes=_vmem_limit(48 * 2 ** 20)),
        name="decode_topk_select",
    )(scores, iq, ik, iw)


def _flash_update(m_ref, s_ref, acc_ref, logits, pv_fn):
    m_old = m_ref[...]
    m_new = jnp.maximum(m_old, jnp.max(logits, axis=1, keepdims=True))
    m_safe = jnp.where(m_new == NEG_INF, 0.0, m_new)
    alpha = jnp.exp2(m_old - m_safe)
    p = jnp.exp2(logits - m_safe)
    s_ref[...] = alpha * s_ref[...] + jnp.sum(p, axis=1, keepdims=True)
    acc_ref[...] = alpha * acc_ref[...] + pv_fn(p.astype(BF16))
    m_ref[...] = m_new


def _flash_finish(m_ref, s_ref, acc_ref, l_self, v_self):
    m_old = m_ref[...]
    m_new = jnp.maximum(m_old, l_self)
    m_safe = jnp.where(m_new == NEG_INF, 0.0, m_new)
    alpha = jnp.exp2(m_old - m_safe)
    p = jnp.exp2(l_self - m_safe)
    return (alpha * acc_ref[...] + p * v_self) / (alpha * s_ref[...] + p)


def _add_near_last_page(logits, near, n):
    if n == 1:
        return logits + near
    return jnp.concatenate([logits[:, :(n - 1) * LANES], logits[:, (n - 1) * LANES:] + near], axis=1)


def _dec_even_kernel(pt_ref, neg_ref, negself_ref, aq_ref, qlat_ref, kvn_ref, latn_ref, near_ref, bself_ref,
                     *refs, n):
    kv_pages, lat_pages = refs[:n], refs[n:2 * n]
    oa_ref, ol_ref = refs[2 * n:2 * n + 2]
    ma_ref, sa_ref, acca_ref, mb_ref, sb_ref, accb_ref = refs[2 * n + 2:]
    g = pl.program_id(1)
    last = pl.num_programs(1) - 1

    @pl.when(g == 0)
    def _():
        ma_ref[...] = jnp.full(ma_ref.shape, NEG_INF, F32)
        mb_ref[...] = jnp.full(mb_ref.shape, NEG_INF, F32)
        sa_ref[...] = jnp.zeros(sa_ref.shape, F32)
        sb_ref[...] = jnp.zeros(sb_ref.shape, F32)
        acca_ref[...] = jnp.zeros(acca_ref.shape, F32)
        accb_ref[...] = jnp.zeros(accb_ref.shape, F32)

    aq = aq_ref[...]
    qlat = qlat_ref[...]
    kv = [p[...].astype(BF16) for p in kv_pages]
    la = jnp.concatenate([_dot(aq, kvp[:LANES]) for kvp in kv], axis=1) + neg_ref[...]
    la = _add_near_last_page(la, jnp.where(g == last, near_ref[...], 0.0), n)

    def pv_a(p):
        return sum(_dot_nt(p[:, j * LANES:(j + 1) * LANES], kv[j][LANES:]) for j in range(n))

    _flash_update(ma_ref, sa_ref, acca_ref, la, pv_a)
    lat = [p[...].astype(BF16) for p in lat_pages]
    zrows = jnp.zeros((256 - KV_LORA - ROPE_B, LANES), BF16)
    lb = jnp.concatenate([_dot(qlat, jnp.concatenate([lp, zrows], axis=0)) for lp in lat], axis=1)

    def pv_b(p):
        return sum(_dot_nt(p[:, j * LANES:(j + 1) * LANES], lat[j][:KV_LORA]) for j in range(n))

    _flash_update(mb_ref, sb_ref, accb_ref, lb, pv_b)

    @pl.when(g == last)
    def _():
        kvn = kvn_ref[...]
        l_self = (jnp.sum(aq.astype(F32) * kvn[:, :LANES], axis=1, keepdims=True) + bself_ref[...]
                  + negself_ref[:, 0:1])
        oa = _flash_finish(ma_ref, sa_ref, acca_ref, l_self, kvn[:, LANES:])
        rowi = lax.broadcasted_iota(jnp.int32, oa.shape, 0)
        oa = jnp.where(rowi < H_A // H_A_KV, oa, pltpu.roll(oa, HD_A, 1))
        oa_ref[...] = oa[:, :HD_A]
        latn = latn_ref[...]
        lb_self = jnp.sum(qlat.astype(F32) * latn, axis=1, keepdims=True)
        ol_ref[...] = _flash_finish(mb_ref, sb_ref, accb_ref, lb_self, latn[:, :KV_LORA]).astype(BF16)


def _dec_even(page_table, neg, negself, aq_s, qlat_s, kv_new, lat_new, near_s, bself, cache_kv_t, cache_lat_t,
              layer):
    nb, npg = page_table.shape
    n = PAGES_PER_STEP if npg % PAGES_PER_STEP == 0 else 1
    page = cache_kv_t.shape[3]
    per_seq = lambda shape: pl.BlockSpec((None,) + shape, lambda b, g, pt: (b,) + (0,) * len(shape))
    const = lambda shape: pl.BlockSpec(shape, lambda b, g, pt: (0,) * len(shape))
    grid_spec = pltpu.PrefetchScalarGridSpec(
        num_scalar_prefetch=1, grid=(nb, npg // n),
        in_specs=[pl.BlockSpec((None, 1, n * page), lambda b, g, pt: (b, 0, g)), per_seq((1, LANES)),
                  per_seq((H_A, LANES)), per_seq((H_B, 256)), per_seq((1, 256)), per_seq((1, 256)),
                  const((H_A, LANES)), const((H_A, 1))]
        + _page_specs((256, page), layer, n) + _page_specs((KV_LORA + ROPE_B, page), layer, n),
        out_specs=[per_seq((H_A, HD_A)), per_seq((H_B, KV_LORA))],
        scratch_shapes=[pltpu.VMEM((H_A, 1), F32), pltpu.VMEM((H_A, 1), F32), pltpu.VMEM((H_A, LANES), F32),
                        pltpu.VMEM((H_B, 1), F32), pltpu.VMEM((H_B, 1), F32), pltpu.VMEM((H_B, KV_LORA), F32)])
    return pl.pallas_call(
        functools.partial(_dec_even_kernel, n=n), grid_spec=grid_spec,
        out_shape=[jax.ShapeDtypeStruct((nb, H_A, HD_A), F32), jax.ShapeDtypeStruct((nb, H_B, KV_LORA), BF16)],
        compiler_params=_params(("parallel", "arbitrary"), 32 * 2 ** 20),
        name="decode_attn_even",
    )(page_table, neg, negself, aq_s, qlat_s, kv_new, lat_new, near_s, bself,
      *([cache_kv_t] * n), *([cache_lat_t] * n))


def _dec_odd_kernel(pt_ref, lam_ref, q_ref, kvn_ref, near_ref, bself_ref, subg_ref, *refs, n, lam_init):
    pages = refs[:n]
    o_ref = refs[n]
    m_ref, s_ref, acc_ref = refs[n + 1:]
    g = pl.program_id(1)
    last = pl.num_programs(1) - 1
    page = pages[0].shape[0] // (2 * H_C_KV)
    width = 2 * HD_C

    @pl.when(g == 0)
    def _():
        m_ref[...] = jnp.full(m_ref.shape, NEG_INF, F32)
        s_ref[...] = jnp.zeros(s_ref.shape, F32)
        acc_ref[...] = jnp.zeros(acc_ref.shape, F32)

    q = q_ref[...]
    row_g0 = lax.broadcasted_iota(jnp.int32, (2 * H_C, width), 0) % H_C < H_C // H_C_KV
    part = lambda p, r: p[pl.ds(r, page, stride=2 * H_C_KV), :].astype(BF16)
    logits = jnp.concatenate(
        [_dot_nt(q[:, :width], part(p, 0)) + _dot_nt(q[:, width:], part(p, 1)) for p in pages], axis=1)
    logits = _add_near_last_page(logits, jnp.where(g == last, near_ref[...], 0.0), n)

    def pv(p):
        v0 = sum(_dot(p[:, j * LANES:(j + 1) * LANES], part(pages[j], 2)) for j in range(n))
        v1 = sum(_dot(p[:, j * LANES:(j + 1) * LANES], part(pages[j], 3)) for j in range(n))
        return jnp.where(row_g0, v0, v1)

    _flash_update(m_ref, s_ref, acc_ref, logits, pv)

    @pl.when(g == last)
    def _():
        kvn = kvn_ref[...]
        l_self = jnp.sum(q.astype(F32) * kvn[:, :2 * width], axis=1, keepdims=True) + bself_ref[...]
        v_self = jnp.where(row_g0, kvn[:, 2 * width:3 * width], kvn[:, 3 * width:])
        o = _flash_finish(m_ref, s_ref, acc_ref, l_self, v_self)
        d = o[:H_C] - lam_ref[0] * o[H_C:]
        o_ref[...] = _rms(d, subg_ref[...]) * (1.0 - lam_init)


def _dec_odd(page_table, lam, q_s, kv_new, near_s, bself, subg, cache_kv_r, layer, lam_init):
    nb, npg = page_table.shape
    n = PAGES_PER_STEP if npg % PAGES_PER_STEP == 0 else 1
    rows, width = cache_kv_r.shape[2:]
    per_seq = lambda shape: pl.BlockSpec((None,) + shape, lambda b, g, pt: (b,) + (0,) * len(shape))
    const = lambda shape: pl.BlockSpec(shape, lambda b, g, pt: (0,) * len(shape))
    grid_spec = pltpu.PrefetchScalarGridSpec(
        num_scalar_prefetch=1, grid=(nb, npg // n),
        in_specs=[pl.BlockSpec(memory_space=pltpu.SMEM), per_seq((2 * H_C, H_C_KV * width)),
                  per_seq((1, 2 * H_C_KV * width)), const((2 * H_C, LANES)), const((2 * H_C, 1)), const((1, width))]
        + _page_specs((rows, width), layer, n),
        out_specs=per_seq((H_C, width)),
        scratch_shapes=[pltpu.VMEM((2 * H_C, 1), F32), pltpu.VMEM((2 * H_C, 1), F32),
                        pltpu.VMEM((2 * H_C, width), F32)])
    return pl.pallas_call(
        functools.partial(_dec_odd_kernel, n=n, lam_init=lam_init), grid_spec=grid_spec,
        out_shape=jax.ShapeDtypeStruct((nb, H_C, width), F32),
        compiler_params=_params(("parallel", "arbitrary"), 32 * 2 ** 20),
        name="decode_attn_odd",
    )(page_table, lam, q_s, kv_new, near_s, bself, subg, *([cache_kv_r] * n))


def _t5_bias(table, rel):
    n = jnp.maximum(rel, 0)
    max_exact = NUM_BUCKETS // 2
    nf = jnp.maximum(n, 1).astype(F32)
    large = max_exact + (jnp.log(nf / max_exact) / math.log(MAX_DISTANCE / max_exact)
                         * (NUM_BUCKETS - max_exact)).astype(jnp.int32)
    bucket = jnp.where(n < max_exact, n, jnp.minimum(large, NUM_BUCKETS - 1))
    return table[bucket].astype(F32)


def _far_distance_is_constant(first_far, max_rel):
    n = np.arange(first_far, max_rel + 1, dtype=np.float64)
    large = NUM_BUCKETS // 2 + np.floor(np.log(n / (NUM_BUCKETS // 2)) / math.log(MAX_DISTANCE / (NUM_BUCKETS // 2))
                                        * (NUM_BUCKETS - NUM_BUCKETS // 2) - 1e-3)
    return bool(np.all(large >= NUM_BUCKETS - 1))


def _near_tiles(table, per):
    r = jnp.arange(Q_BLOCK, dtype=jnp.int32)[:, None]
    c = jnp.arange(Q_BLOCK, dtype=jnp.int32)[None, :]
    far = table[NUM_BUCKETS - 1].astype(F32)
    before = jnp.transpose(_t5_bias(table, Q_BLOCK + r - c) - far, (2, 0, 1)) * LOG2E
    diag = jnp.transpose(_t5_bias(table, r - c) - far, (2, 0, 1)) * LOG2E
    z = jnp.zeros((table.shape[1], per - 1, Q_BLOCK, Q_BLOCK), F32)
    return jnp.concatenate([z, before[:, None], diag[:, None], z], axis=1)


def _near_rows(table, page):
    rel = page - jnp.arange(page, dtype=jnp.int32)
    far = table[NUM_BUCKETS - 1].astype(F32)
    return ((_t5_bias(table, rel) - far) * LOG2E).T, ((_t5_bias(table, jnp.zeros((1,), jnp.int32)) - far) * LOG2E).T


def _rope_tables(pos, rows):
    half = ROPE_B // 2
    inv = ROPE_THETA ** (-jnp.arange(half, dtype=F32) / half)
    ang = pos.astype(F32)[:, None] * inv[None, :]
    cos, sin = jnp.cos(ang), jnp.sin(ang)
    z = jnp.zeros_like(cos)
    pad = jnp.zeros((pos.shape[0], LANES - ROPE_B), F32)
    tabs = [jnp.concatenate([cos, cos, pad], 1), jnp.concatenate([sin, sin, pad], 1),
            jnp.concatenate([-sin, z, pad], 1), jnp.concatenate([z, sin, pad], 1)]
    return [jnp.broadcast_to(t, (rows, LANES)) if t.shape[0] != rows else t for t in tabs]


def _prep_ab_weights(w_in, w_uq, w_uk, w_uv):
    d = w_in.shape[0]
    o = np.cumsum([0, H_A * HD_A, H_A_KV * HD_A, H_A_KV * HD_A, H_IDX * HD_IDX, HD_IDX, H_IDX, Q_LORA, KV_LORA,
                   ROPE_B])
    sec = lambda i: w_in[:, o[i]:o[i + 1]]
    w1 = jnp.concatenate([sec(0), sec(1), sec(2), sec(3), sec(6), sec(7), sec(8), sec(5),
                          jnp.zeros((d, GRP_IK - GRP_IW - H_IDX), w_in.dtype), sec(4)], axis=1).astype(BF16)
    assert w1.shape[1] == AB_COLS
    uq = w_uq.reshape(Q_LORA, H_B, NOPE_B + ROPE_B)
    nope = uq[:, :, :NOPE_B].reshape(Q_LORA, H_B * NOPE_B)
    r = uq[:, :, NOPE_B:]
    half = ROPE_B // 2
    rot = jnp.concatenate([-r[:, :, half:], r[:, :, :half]], axis=-1)
    padl = lambda a: jnp.pad(a, ((0, 0), (0, 0), (0, LANES - ROPE_B))).reshape(Q_LORA, H_B * LANES)
    w2 = jnp.concatenate([nope, padl(r), padl(rot)], axis=1).astype(BF16)
    eye = jnp.eye(H_B, dtype=w_uk.dtype)
    wuk = jnp.einsum("chn,hg->hngc", w_uk, eye).reshape(H_B * NOPE_B, H_B * KV_LORA).astype(BF16)
    wv = jnp.transpose(w_uv, (1, 0, 2))
    zl = jnp.zeros_like(wv)
    even = (jnp.arange(H_B) % 2 == 0)[:, None, None]
    wuv = jnp.where(even, jnp.concatenate([wv, zl], -1), jnp.concatenate([zl, wv], -1)).astype(BF16)
    wuv_bd = jnp.einsum("chv,hg->hcgv", w_uv, eye).reshape(H_B * KV_LORA, H_B * V_B).astype(BF16)
    return w1, w2, wuk, wuv, wuv_bd


def _lambda(lambda_qk, layer):
    lam_init = 0.8 - 0.6 * math.exp(-0.3 * layer)
    lq = lambda_qk.astype(F32)
    lam = jnp.exp(jnp.sum(lq[0] * lq[1])) - jnp.exp(jnp.sum(lq[2] * lq[3])) + lam_init
    return lam.reshape(1), lam_init


def _run_prompt(x, P):
    nb, t, d = x.shape
    assert t % Q_BLOCK == 0
    m = nb * t
    topk = min(TOPK_MAX, t // 4)
    xf = x.reshape(m, d)
    tabs = _rope_tables(jnp.arange(t, dtype=jnp.int32), t)
    a_kv_l, a_idx_l, b_lat_l, c_kv_l = [], [], [], []
    depth = P["attn_norm"].shape[0]
    for l in range(depth):
        last = l == depth - 1
        if l % 2 == 0:
            e = l // 2
            w1, w2, wuk, wuv, _ = P["ab"][e]
            (aq, akv, iq, ik, iw, qlat, blat, av2, latc, ik2, akt, iwt, latt) = _ab_proj(
                xf, P["attn_norm"][l][None], w1, P["g_q"][e][None], w2, wuk, P["g_kv"][e][None], tabs, nb, t, True)
            o = _attn_even(aq, iq, iwt, qlat, akt, av2, ik2, latt, latc, P["near_a"], wuv, nb, t, topk)
            wo = P["w_o_ab"][e]
            a_kv_l.append(akv.reshape(nb, t, 2, H_A_KV, HD_A))
            a_idx_l.append(ik.reshape(nb, t, HD_IDX))
            b_lat_l.append(blat.reshape(nb, t, KV_LORA + ROPE_B))
        else:
            j = l // 2
            lam, lam_init = _lambda(P["lambda_qk"][j], l)
            q, ckv, ckt, cv = _c_proj(xf, P["attn_norm"][l][None], P["w_in_c"][j], nb, t, True)
            o = _attn_odd(lam, q, ckt, cv, P["near_c"], P["sub_norm_c"][j][None], nb, t, lam_init)
            wo = P["w_o_c"][j]
            c_kv_l.append(ckv.reshape(nb, t, 2, H_C_KV, 2 * HD_C))
        xf = _out_ffn(xf, o, wo, P["ffn_norm"][l][None], P["w_gate_up"][l], P["w_down"][l], P["final_norm"][None],
                      last)
    return (xf.reshape(nb, t, d), jnp.stack(a_kv_l), jnp.stack(a_idx_l), jnp.stack(b_lat_l), jnp.stack(c_kv_l))


def _run_sample(x, P, caches, page_table):
    cache_a_kv, cache_a_idx, cache_b_latent, cache_c_kv = caches
    nb, t, d = x.shape
    assert t == 1
    npg = page_table.shape[1]
    page = cache_a_idx.shape[2]
    past = npg * page
    topk = min(TOPK_MAX, (past + t) // 4)
    assert _far_distance_is_constant(page + 1, past)
    xf = x.reshape(nb, d)
    tabs = _rope_tables(jnp.full((1,), past, jnp.int32), nb)
    idx_t = jnp.transpose(cache_a_idx, (0, 1, 3, 2))
    kv_t = jnp.transpose(cache_a_kv, (0, 1, 3, 4, 5, 2)).reshape(cache_a_kv.shape[:2] + (-1, page))
    lat_t = jnp.transpose(cache_b_latent, (0, 1, 3, 2))
    ckv_r = cache_c_kv.reshape(cache_c_kv.shape[:2] + (-1, cache_c_kv.shape[-1]))
    a_kv_l, a_idx_l, b_lat_l, c_kv_l = [], [], [], []
    depth = P["attn_norm"].shape[0]
    gsz_a = H_A // H_A_KV
    head_lo_a = (jnp.arange(H_A) < gsz_a)[None, :, None]
    for l in range(depth):
        last = l == depth - 1
        if l % 2 == 0:
            e = l // 2
            w1, w2, wuk, _, wuv_bd = P["ab"][e]
            aq, akv, iq, ik, iw, qlat, blat = _ab_proj(
                xf, P["attn_norm"][l][None], w1, P["g_q"][e][None], w2, wuk, P["g_kv"][e][None], tabs, nb, t, False)
            iq_s = jnp.pad(iq.reshape(nb, H_IDX, HD_IDX), ((0, 0), (0, 8 - H_IDX), (0, 0)))
            iw_s = jnp.pad(iw, ((0, 0), (0, 8 - H_IDX)))[:, :, None]
            scores = _dec_scores(page_table, iq_s, iw_s, idx_t, e)
            neg, negself = _dec_select(scores.reshape(nb, past), iq, ik, iw, topk)
            aq3 = aq.reshape(nb, H_A, HD_A)
            zq = jnp.zeros_like(aq3)
            aq_s = jnp.where(head_lo_a, jnp.concatenate([aq3, zq], -1), jnp.concatenate([zq, aq3], -1))
            lat_new = jnp.pad(blat, ((0, 0), (0, 256 - blat.shape[1])))
            oa, olat = _dec_even(page_table, neg.reshape(nb, 1, past), negself.reshape(nb, 1, LANES), aq_s,
                                 qlat.reshape(nb, H_B, 256), akv.reshape(nb, 1, 256), lat_new.reshape(nb, 1, 256),
                                 P["near_a_s"], P["bself_a"], kv_t, lat_t, e)
            ob = _mm(olat.reshape(nb, H_B * KV_LORA), wuv_bd, BF16)
            o = jnp.concatenate([oa.reshape(nb, H_A * HD_A).astype(BF16), ob], axis=1)
            wo = P["w_o_ab"][e]
            a_kv_l.append(akv.reshape(nb, t, 2, H_A_KV, HD_A))
            a_idx_l.append(ik.reshape(nb, t, HD_IDX))
            b_lat_l.append(blat.reshape(nb, t, KV_LORA + ROPE_B))
        else:
            j = l // 2
            lam, lam_init = _lambda(P["lambda_qk"][j], l)
            q, ckv = _c_proj(xf, P["attn_norm"][l][None], P["w_in_c"][j], nb, t, False)
            q4 = q.reshape(nb, H_C, 2, HD_C)
            sel = ((jnp.arange(H_C)[:, None, None, None] // (H_C // H_C_KV) == jnp.arange(H_C_KV)[None, None, :, None])
                   & (jnp.arange(2)[None, :, None, None] == jnp.arange(2)[None, None, None, :]))
            q_s = jnp.where(sel[None, :, :, :, :, None], q4[:, :, :, None, None, :], jnp.zeros((), q.dtype))
            q_s = jnp.transpose(q_s, (0, 2, 1, 3, 4, 5)).reshape(nb, 2 * H_C, H_C_KV * 2 * HD_C)
            oc = _dec_odd(page_table, lam, q_s, ckv.reshape(nb, 1, -1), P["near_c_s"], P["bself_c"],
                          P["sub_norm_c"][j][None], ckv_r, j, lam_init)
            o = oc.reshape(nb, H_C * 2 * HD_C).astype(BF16)
            wo = P["w_o_c"][j]
            c_kv_l.append(ckv.reshape(nb, t, 2, H_C_KV, 2 * HD_C))
        xf = _out_ffn(xf, o, wo, P["ffn_norm"][l][None], P["w_gate_up"][l], P["w_down"][l], P["final_norm"][None],
                      last)
    return (xf.reshape(nb, t, d), jnp.stack(a_kv_l), jnp.stack(a_idx_l), jnp.stack(b_lat_l), jnp.stack(c_kv_l))


def kernel(x_prompt, x_sample, cache_a_kv, cache_a_idx, cache_b_latent, cache_c_kv, page_table, attn_norm, w_in_ab,
           w_uq, g_q, g_kv, w_uk, w_uv, w_o_ab, rel_bias, w_in_c, lambda_qk, sub_norm_c, w_o_c, ffn_norm, w_gate_up,
           w_down, final_norm):
    page = cache_a_idx.shape[2]
    assert page == LANES
    assert _far_distance_is_constant(Q_BLOCK + 1, max(x_prompt.shape[1], Q_BLOCK + 1))
    tab_a, tab_c = rel_bias[:, :H_A], rel_bias[:, H_A:]
    per = (x_prompt.shape[1] // Q_BLOCK) // _key_classes(x_prompt.shape[1])
    near_a_s, bself_a = _near_rows(tab_a, page)
    near_c_s, bself_c = _near_rows(tab_c, page)
    P = dict(
        attn_norm=attn_norm, g_q=g_q, g_kv=g_kv, ffn_norm=ffn_norm, final_norm=final_norm, lambda_qk=lambda_qk,
        sub_norm_c=sub_norm_c,
        ab=[_prep_ab_weights(w_in_ab[e], w_uq[e], w_uk[e], w_uv[e]) for e in range(w_in_ab.shape[0])],
        w_o_ab=w_o_ab.astype(BF16), w_in_c=w_in_c.astype(BF16), w_o_c=w_o_c.astype(BF16),
        w_gate_up=w_gate_up.astype(BF16), w_down=w_down.astype(BF16),
        near_a=_near_tiles(tab_a, per), near_c=_near_tiles(tab_c, per),
        near_a_s=near_a_s, bself_a=bself_a,
        near_c_s=jnp.concatenate([near_c_s, near_c_s], 0), bself_c=jnp.concatenate([bself_c, bself_c], 0),
    )
    y_p, akv_p, aidx_p, blat_p, ckv_p = _run_prompt(x_prompt, P)
    y_s, akv_s, aidx_s, blat_s, ckv_s = _run_sample(
        x_sample, P, (cache_a_kv, cache_a_idx, cache_b_latent, cache_c_kv), page_table)
    return (y_p, y_s, akv_p, aidx_p, blat_p, ckv_p, akv_s, aidx_s, blat_s, ckv_s)
```

```python
import functools
import math

import numpy as np
import jax
import jax.numpy as jnp
from jax import lax
from jax.experimental import pallas as pl
from jax.experimental.pallas import tpu as pltpu

EPS = 1e-6
NUM_BUCKETS = 32
MAX_DISTANCE = 128
H_A, HD_A, H_A_KV = 8, 64, 2
H_IDX, HD_IDX = 4, 64
TOPK_MAX = 256
H_B, NOPE_B, ROPE_B, V_B = 8, 64, 32, 64
Q_LORA, KV_LORA = 256, 128
ROPE_THETA = 10000.0
MLA_SCALE = (NOPE_B + ROPE_B) ** -0.5
H_C, HD_C, H_C_KV = 8, 64, 2

LANES = 128
V7X_VMEM_BYTES = 64 * 2 ** 20

Q_BLOCK = LANES
F32 = jnp.float32
BF16 = jnp.bfloat16
NEG_INF = float("-inf")
SUM_ROWS = 64
LOG2E = 1.4426950408889634
INT_MIN = -2 ** 31


def _vmem_limit(nbytes):
    return int(min(max(nbytes, 32 * 2 ** 20), V7X_VMEM_BYTES - 8 * 2 ** 20))


def _params(semantics, vmem_bytes):
    return pltpu.CompilerParams(dimension_semantics=semantics, vmem_limit_bytes=_vmem_limit(vmem_bytes))


def _rms(x, g):
    return x * lax.rsqrt(jnp.mean(x * x, axis=-1, keepdims=True) + EPS) * g


def _dot(a, b):
    return jnp.dot(a, b, preferred_element_type=F32)


def _dot_nt(a, b):
    return lax.dot_general(a, b, (((1,), (1,)), ((), ())), preferred_element_type=F32)


def _const_spec(shape):
    nd = len(shape)
    return pl.BlockSpec(shape, lambda *_: (0,) * nd, pipeline_mode=pl.Buffered(1))


def _ordered_bits_to_float(u):
    key = jnp.maximum(u ^ jnp.int32(INT_MIN), jnp.int32(-0x7F800001))
    bits = jnp.where(key >= 0, key, key ^ jnp.int32(0x7FFFFFFF))
    return lax.bitcast_convert_type(bits, F32)


def _topk_neg_mask(score, idx, k, axis):
    n = score.shape[axis]
    state = (score.shape[0], 1) if axis == 1 else (1, score.shape[1])
    kf = jnp.float32(k)

    def reduce_sum(x):
        if axis == 0 and n % SUM_ROWS == 0 and n > SUM_ROWS:
            x = jnp.sum(x.reshape(n // SUM_ROWS, SUM_ROWS, x.shape[1]), axis=0)
        return jnp.sum(x, axis=axis, keepdims=True)

    def count(mask):
        return reduce_sum(jnp.where(mask, 1.0, 0.0))

    def value_step(it, u):
        c = u | lax.shift_left(jnp.int32(1), 31 - it)
        return jnp.where(count(score >= _ordered_bits_to_float(c)) >= kf, c, u)

    u = lax.fori_loop(0, 32, value_step, jnp.zeros(state, jnp.int32))
    thr = _ordered_bits_to_float(u)
    gt = score > thr
    eq = score == thr
    need = kf - count(gt)
    n_eq = count(eq)
    nbits = int(n).bit_length()

    def tie_search():
        eqf = jnp.where(eq, 1.0, 0.0)

        def idx_step(it, j):
            c = j | lax.shift_left(jnp.int32(1), nbits - 1 - it)
            m = reduce_sum(jnp.where(idx < c, eqf, 0.0))
            return jnp.where(m <= need, c, j)

        return lax.fori_loop(0, nbits, idx_step, jnp.zeros(state, jnp.int32))

    overflow = jnp.max(n_eq - need) > 0.0
    jcut = lax.cond(overflow, tie_search, lambda: jnp.full(state, 2 ** nbits - 1, jnp.int32))
    keep_tie = jnp.where(idx < jcut, 0.0, NEG_INF)
    return jnp.where(gt, 0.0, jnp.where(eq, keep_tie, NEG_INF))


AB_AQ, AB_AK, AB_AV, AB_IQ, AB_CQ, AB_CKV, AB_GRP, AB_COLS = 0, 512, 640, 768, 1024, 1280, 1408, 1536
GRP_KR, GRP_IW, GRP_IK = 0, 32, 64


def _ab_proj_kernel(x_ref, g_ref, w1_ref, gq_ref, w2_ref, wuk_ref, gkv_ref, cos_ref, sin_ref, nsl_ref, shi_ref,
                    aq_ref, akv_ref, iq_ref, ik_ref, iw_ref, qlat_ref, blat_ref, *t_refs, emit_t):
    h = _rms(x_ref[...], g_ref[...]).astype(BF16)
    y = _dot(h, w1_ref[...])
    aq_ref[...] = (y[:, AB_AQ:AB_AK] * (HD_A ** -0.5 * LOG2E)).astype(BF16)
    akv_ref[...] = y[:, AB_AK:AB_IQ]
    iq_ref[...] = (y[:, AB_IQ:AB_CQ] * HD_IDX ** -0.5).astype(BF16)
    grp = y[:, AB_GRP:AB_COLS]
    ik_ref[...] = grp[:, GRP_IK:GRP_IK + HD_IDX]
    iw_ref[...] = grp[:, GRP_IW:GRP_IW + H_IDX] * H_IDX ** -0.5
    cq = _rms(y[:, AB_CQ:AB_CKV], gq_ref[...]).astype(BF16)
    q2 = _dot(cq, w2_ref[...])
    n_nope = H_B * NOPE_B
    n_pad = H_B * LANES
    qc = _dot(q2[:, :n_nope].astype(BF16), wuk_ref[...])
    cos8 = jnp.concatenate([cos_ref[...]] * H_B, axis=1)
    sin8 = jnp.concatenate([sin_ref[...]] * H_B, axis=1)
    qr = q2[:, n_nope:n_nope + n_pad] * cos8 + q2[:, n_nope + n_pad:] * sin8
    pieces = []
    for hh in range(H_B):
        pieces += [qc[:, hh * LANES:(hh + 1) * LANES], qr[:, hh * LANES:(hh + 1) * LANES]]
    qlat_ref[...] = (jnp.concatenate(pieces, axis=1) * (MLA_SCALE * LOG2E)).astype(BF16)
    latc = _rms(y[:, AB_CKV:AB_GRP], gkv_ref[...])
    half = ROPE_B // 2
    kr = (grp * cos_ref[...] + pltpu.roll(grp, LANES - half, 1) * nsl_ref[...]
          + pltpu.roll(grp, half, 1) * shi_ref[...])
    blat_ref[:, :KV_LORA] = latc
    blat_ref[:, KV_LORA:] = kr[:, :ROPE_B]
    if emit_t:
        av2_ref, latc_ref, ik2_ref, akt_ref, iwt_ref, latt_ref = t_refs
        lane = lax.broadcasted_iota(jnp.int32, (y.shape[0], LANES), 1)
        v = y[:, AB_AV:AB_IQ]
        vs = pltpu.roll(v, HD_A, 1)
        av2_ref[...] = jnp.concatenate([jnp.where(lane < HD_A, v, vs), jnp.where(lane < HD_A, vs, v)],
                                       axis=1).astype(BF16)
        latc_ref[...] = latc.astype(BF16)
        kt = y[:, AB_AK:AB_AV].T
        akt_ref[...] = jnp.concatenate([kt[:HD_A], kt[:HD_A], kt[HD_A:], kt[HD_A:]], axis=0).astype(BF16)
        ik2_ref[...] = jnp.where(lane < HD_IDX, pltpu.roll(grp, HD_IDX, 1), grp).astype(BF16)
        iwt_ref[...] = grp.T[GRP_IW:GRP_IW + 8] * H_IDX ** -0.5
        latt_ref[...] = jnp.concatenate([latc.T, kr.T], axis=0).astype(BF16)


def _ab_proj(x, g, w1, gq, w2, wuk, gkv, rope_tabs, n_batch, t, emit_t):
    m = x.shape[0]
    tm = min(512, t) if emit_t else m
    assert m % tm == 0 and (not emit_t or t % tm == 0)
    tpb = max(t // tm, 1)
    row = lambda cols: pl.BlockSpec((tm, cols), lambda i: (i, 0))
    tab = pl.BlockSpec((tm, LANES), (lambda i: (i % tpb, 0)) if emit_t else (lambda i: (0, 0)))
    tsp = lambda rows: pl.BlockSpec((None, rows, tm), lambda i: (i // tpb, 0, i % tpb))
    out_shape = [jax.ShapeDtypeStruct((m, 512), BF16), jax.ShapeDtypeStruct((m, 256), F32),
                 jax.ShapeDtypeStruct((m, 256), BF16), jax.ShapeDtypeStruct((m, HD_IDX), F32),
                 jax.ShapeDtypeStruct((m, H_IDX), F32), jax.ShapeDtypeStruct((m, H_B * 256), BF16),
                 jax.ShapeDtypeStruct((m, KV_LORA + ROPE_B), F32)]
    out_specs = [row(512), row(256), row(256), row(HD_IDX), row(H_IDX), row(H_B * 256), row(KV_LORA + ROPE_B)]
    if emit_t:
        out_shape += [jax.ShapeDtypeStruct((m, 256), BF16), jax.ShapeDtypeStruct((m, KV_LORA), BF16),
                      jax.ShapeDtypeStruct((m, 2 * HD_IDX), BF16),
                      jax.ShapeDtypeStruct((n_batch, 256, t), BF16), jax.ShapeDtypeStruct((n_batch, 8, t), F32),
                      jax.ShapeDtypeStruct((n_batch, 256, t), BF16)]
        out_specs += [row(256), row(KV_LORA), row(2 * HD_IDX), tsp(256), tsp(8), tsp(256)]
    return pl.pallas_call(
        functools.partial(_ab_proj_kernel, emit_t=emit_t),
        grid=(m // tm,),
        in_specs=[row(x.shape[1]), _const_spec(g.shape), _const_spec(w1.shape), _const_spec(gq.shape),
                  _const_spec(w2.shape), _const_spec(wuk.shape), _const_spec(gkv.shape), tab, tab, tab, tab],
        out_specs=out_specs, out_shape=out_shape,
        compiler_params=_params(("parallel",), 40 * 2 ** 20),
        name="ab_proj_prompt" if emit_t else "ab_proj_sample",
    )(x, g, w1, gq, w2, wuk, gkv, *rope_tabs)


def _c_proj_kernel(x_ref, g_ref, w_ref, q_ref, ckv_ref, *t_refs, emit_t):
    h = _rms(x_ref[...], g_ref[...]).astype(BF16)
    y = _dot(h, w_ref[...])
    nq = H_C * 2 * HD_C
    nk = H_C_KV * 2 * HD_C
    q_ref[...] = (y[:, :nq] * (HD_C ** -0.5 * LOG2E)).astype(BF16)
    ckv_ref[...] = y[:, nq:]
    if emit_t:
        ckt_ref, cv_ref = t_refs
        ckt_ref[...] = y[:, nq:nq + nk].T.astype(BF16)
        cv_ref[...] = y[:, nq + nk:].astype(BF16)


def _c_proj(x, g, w, n_batch, t, emit_t):
    m = x.shape[0]
    tm = min(512, t) if emit_t else m
    tpb = max(t // tm, 1)
    nq, nk = H_C * 2 * HD_C, H_C_KV * 2 * HD_C
    row = lambda cols: pl.BlockSpec((tm, cols), lambda i: (i, 0))
    out_shape = [jax.ShapeDtypeStruct((m, nq), BF16), jax.ShapeDtypeStruct((m, 2 * nk), F32)]
    out_specs = [row(nq), row(2 * nk)]
    if emit_t:
        out_shape += [jax.ShapeDtypeStruct((n_batch, nk, t), BF16), jax.ShapeDtypeStruct((m, nk), BF16)]
        out_specs += [pl.BlockSpec((None, nk, tm), lambda i: (i // tpb, 0, i % tpb)), row(nk)]
    return pl.pallas_call(
        functools.partial(_c_proj_kernel, emit_t=emit_t),
        grid=(m // tm,),
        in_specs=[row(x.shape[1]), _const_spec(g.shape), _const_spec(w.shape)],
        out_specs=out_specs, out_shape=out_shape,
        compiler_params=_params(("parallel",), 32 * 2 ** 20),
        name="c_proj_prompt" if emit_t else "c_proj_sample",
    )(x, g, w)


def _out_ffn_kernel(x_ref, o_ref, wo_ref, g_ref, wgu_ref, wd_ref, fg_ref, out_ref, *, d_ff, chunk, final):
    x1 = x_ref[...] + _dot(o_ref[...], wo_ref[...])
    h = _rms(x1, g_ref[...]).astype(BF16)
    acc = x1
    for c in range(d_ff // chunk):
        gate = _dot(h, wgu_ref[:, c * chunk:(c + 1) * chunk])
        up = _dot(h, wgu_ref[:, d_ff + c * chunk:d_ff + (c + 1) * chunk])
        act = gate * jax.nn.sigmoid(gate) * up
        acc = acc + _dot(act.astype(BF16), wd_ref[c * chunk:(c + 1) * chunk, :])
    if final:
        acc = _rms(acc, fg_ref[...])
    out_ref[...] = acc


def _out_ffn(x, o, wo, g, wgu, wd, fg, final):
    m, d = x.shape
    d_ff = wd.shape[0]
    chunk = 256 if d_ff % 256 == 0 else LANES
    assert d_ff % chunk == 0
    tm = min(512, m)
    assert m % tm == 0
    row = lambda cols: pl.BlockSpec((tm, cols), lambda i: (i, 0))
    return pl.pallas_call(
        functools.partial(_out_ffn_kernel, d_ff=d_ff, chunk=chunk, final=final),
        grid=(m // tm,),
        in_specs=[row(d), row(o.shape[1]), _const_spec(wo.shape), _const_spec(g.shape), _const_spec(wgu.shape),
                  _const_spec(wd.shape), _const_spec(fg.shape)],
        out_specs=row(d), out_shape=jax.ShapeDtypeStruct((m, d), F32),
        compiler_params=_params(("parallel",), 52 * 2 ** 20),
        name="out_ffn",
    )(x, o, wo, g, wgu, wd, fg)


def _mm_kernel(a_ref, b_ref, o_ref):
    o_ref[...] = _dot(a_ref[...], b_ref[...]).astype(o_ref.dtype)


def _mm(a, b, out_dtype):
    return pl.pallas_call(_mm_kernel, out_shape=jax.ShapeDtypeStruct((a.shape[0], b.shape[1]), out_dtype),
                          name="small_matmul")(a, b)


def _softmax_parts(logits):
    m = jnp.max(logits, axis=1, keepdims=True)
    p = jnp.exp2(logits - m)
    return p.astype(BF16), 1.0 / jnp.sum(p, axis=1, keepdims=True)


def _add_to_blocks(logits, addends):
    nb = logits.shape[1] // Q_BLOCK
    cols = [logits[:, j * Q_BLOCK:(j + 1) * Q_BLOCK] for j in range(nb)]
    return jnp.concatenate([cols[j] + addends[j] if j in addends else cols[j] for j in range(nb)], axis=1)


def _pipelined(units, ahead=2):
    pending = [units[i][0]() for i in range(min(ahead, len(units)))]
    for i, (_, finish) in enumerate(units):
        if i + ahead < len(units):
            pending.append(units[i + ahead][0]())
        finish(pending.pop(0))


def _near_addends(near_ref, head, r, first, per):
    return {first - 1 + jj: near_ref[head, per - 1 - r + jj] for jj in range(per + 1) if first - 1 + jj >= 0}


def _diag_causal(qb, first, per):
    w = per * Q_BLOCK
    row = lax.broadcasted_iota(jnp.int32, (Q_BLOCK, w), 0)
    col = lax.broadcasted_iota(jnp.int32, (Q_BLOCK, w), 1)
    return jnp.where(first * Q_BLOCK + col <= qb * Q_BLOCK + row, 0.0, NEG_INF)


def _attn_even_body(cls, per, qb, topk, aq_ref, iq_ref, iwt_ref, qlat_ref, akt_ref, av2_ref, ik2_ref, latt_ref,
                    latc_ref, near_ref, wuv_ref, o_ref):
    first = cls * per
    s = (cls + 1) * per * Q_BLOCK
    r = qb - first
    lo = lax.broadcasted_iota(jnp.int32, (Q_BLOCK, LANES), 1) < HD_A
    zero = jnp.zeros((), BF16)
    key = lax.broadcasted_iota(jnp.int32, (s, Q_BLOCK), 0)
    adm_t = key <= qb * Q_BLOCK + lax.broadcasted_iota(jnp.int32, (s, Q_BLOCK), 1)
    iq = iq_ref[...]
    iwt = iwt_ref[...]
    ik2 = ik2_ref[:s, :]
    score_t = jnp.zeros((s, Q_BLOCK), F32)
    for hp in range(H_IDX // 2):
        grp = iq[:, hp * LANES:(hp + 1) * LANES]
        q2 = jnp.concatenate([jnp.where(lo, grp, zero), jnp.where(lo, zero, grp)], axis=0)
        sc = jnp.maximum(_dot_nt(ik2, q2), 0.0)
        score_t = (score_t + sc[:, :Q_BLOCK] * iwt[2 * hp:2 * hp + 1, :]
                   + sc[:, Q_BLOCK:] * iwt[2 * hp + 1:2 * hp + 2, :])
    score_t = jnp.where(adm_t, score_t, NEG_INF)
    neg_sel = jnp.where(adm_t, _topk_neg_mask(score_t, key, topk, 0), NEG_INF).T
    aq = aq_ref[...]
    gsz = H_A // H_A_KV
    diag = _diag_causal(qb, first, per)
    causal = {first + j: diag[:, j * Q_BLOCK:(j + 1) * Q_BLOCK] for j in range(per)}
    base = H_A * HD_A
    held = {}

    def issue_a(head):
        g, pair, e = head // gsz, (head % gsz) // 2, head % 2
        grp = aq[:, (g * gsz // 2 + pair) * LANES:(g * gsz // 2 + pair + 1) * LANES]
        qm = jnp.where(lo, grp, zero) if e == 0 else jnp.where(lo, zero, grp)
        logits = _dot(qm, akt_ref[g * LANES:(g + 1) * LANES, :s]) + neg_sel
        return _add_to_blocks(logits, _near_addends(near_ref, head, r, first, per))

    def finish_a(head, logits):
        g = head // gsz
        p, inv = _softmax_parts(logits)
        o = _dot(p, av2_ref[:s, g * LANES:(g + 1) * LANES]) * inv
        if head % 2 == 0:
            held["a"] = o
        else:
            o_ref[:, (head // 2) * LANES:(head // 2 + 1) * LANES] = jnp.where(lo, held.pop("a"), o).astype(BF16)

    def issue_b(head):
        return _add_to_blocks(_dot(qlat_ref[:, head * 256:(head + 1) * 256], latt_ref[:, :s]), causal)

    def finish_b(head, logits):
        p, inv = _softmax_parts(logits)
        olat = (_dot(p, latc_ref[:s, :]) * inv).astype(BF16)
        part = _dot(olat, wuv_ref[head])
        if head % 2 == 0:
            held["b"] = part
        else:
            col0 = base + (head // 2) * LANES
            o_ref[:, col0:col0 + LANES] = (held.pop("b") + part).astype(BF16)

    _pipelined([(functools.partial(issue_a, h), functools.partial(finish_a, h)) for h in range(H_A)]
               + [(functools.partial(issue_b, h), functools.partial(finish_b, h)) for h in range(H_B)])


def _attn_even_kernel(aq_ref, iq_ref, iwt_ref, qlat_ref, akt_ref, av2_ref, ik2_ref, latt_ref, latc_ref, near_ref,
                      wuv_ref, o_ref, *, t, n_cls, topk):
    qb = pl.program_id(1)
    per = (t // Q_BLOCK) // n_cls
    for c in range(n_cls):
        @pl.when(qb // per == c)
        def _(c=c):
            _attn_even_body(c, per, qb, topk, aq_ref, iq_ref, iwt_ref, qlat_ref, akt_ref, av2_ref, ik2_ref,
                            latt_ref, latc_ref, near_ref, wuv_ref, o_ref)


EVEN_CLASSES = 4
ODD_CLASSES = 8


def _key_classes(t, want):
    nqb = t // Q_BLOCK
    while nqb % want:
        want //= 2
    return want


def _attn_even(aq, iq, iwt, qlat, akt, av2, ik2, latt, latc, near, wuv, n_batch, t, topk):
    nqb = t // Q_BLOCK
    qrow = lambda cols: pl.BlockSpec((Q_BLOCK, cols), lambda b, i: (b * nqb + i, 0))
    krow = lambda cols: pl.BlockSpec((t, cols), lambda b, i: (b, 0))
    kcol = lambda rows: pl.BlockSpec((None, rows, t), lambda b, i: (b, 0, 0))
    return pl.pallas_call(
        functools.partial(_attn_even_kernel, t=t, n_cls=_key_classes(t, EVEN_CLASSES), topk=topk),
        grid=(n_batch, nqb),
        in_specs=[qrow(512), qrow(256), pl.BlockSpec((None, 8, Q_BLOCK), lambda b, i: (b, 0, i)), qrow(H_B * 256),
                  kcol(256), krow(256), krow(2 * HD_IDX), kcol(256), krow(KV_LORA), _const_spec(near.shape),
                  _const_spec(wuv.shape)],
        out_specs=qrow(1024), out_shape=jax.ShapeDtypeStruct((n_batch * t, 1024), BF16),
        compiler_params=_params(("parallel", "arbitrary"), 48 * 2 ** 20),
        name="attn_even_prompt",
    )(aq, iq, iwt, qlat, akt, av2, ik2, latt, latc, near, wuv)


def _attn_odd_body(cls, per, qb, lam, lam_init, q_ref, ckt_ref, cv_ref, near_ref, subg_ref, o_ref):
    first = cls * per
    s = (cls + 1) * per * Q_BLOCK
    r = qb - first
    diag = _diag_causal(qb, first, per)
    lo = lax.broadcasted_iota(jnp.int32, (Q_BLOCK, LANES), 1) < HD_C
    zero = jnp.zeros((), BF16)
    gsz = H_C // H_C_KV

    def qk(head, e):
        g = head // gsz
        grp = q_ref[:, head * LANES:(head + 1) * LANES]
        qm = jnp.where(lo, grp, zero) if e == 0 else jnp.where(lo, zero, grp)
        addends = _near_addends(near_ref, head, r, first, per)
        for j in range(per):
            addends[first + j] = addends[first + j] + diag[:, j * Q_BLOCK:(j + 1) * Q_BLOCK]
        return _add_to_blocks(_dot(qm, ckt_ref[g * LANES:(g + 1) * LANES, :s]), addends)

    def pv(head, logits):
        g = head // gsz
        p, inv = _softmax_parts(logits)
        return _dot(p, cv_ref[:s, g * LANES:(g + 1) * LANES]) * inv

    held = {}

    def finish(head, e, logits):
        o = pv(head, logits)
        if e == 0:
            held["o"] = o
        else:
            o = held.pop("o") - lam * o
            o = _rms(o, subg_ref[...]) * (1.0 - lam_init)
            o_ref[:, head * LANES:(head + 1) * LANES] = o.astype(BF16)

    _pipelined([(functools.partial(qk, h, e), functools.partial(finish, h, e)) for h in range(H_C) for e in range(2)])


def _attn_odd_kernel(lam_ref, q_ref, ckt_ref, cv_ref, near_ref, subg_ref, o_ref, *, t, n_cls, lam_init):
    qb = pl.program_id(1)
    per = (t // Q_BLOCK) // n_cls
    lam = lam_ref[0]
    for c in range(n_cls):
        @pl.when(qb // per == c)
        def _(c=c):
            _attn_odd_body(c, per, qb, lam, lam_init, q_ref, ckt_ref, cv_ref, near_ref, subg_ref, o_ref)


def _attn_odd(lam, q, ckt, cv, near, subg, n_batch, t, lam_init):
    nqb = t // Q_BLOCK
    qrow = lambda cols: pl.BlockSpec((Q_BLOCK, cols), lambda b, i: (b * nqb + i, 0))
    return pl.pallas_call(
        functools.partial(_attn_odd_kernel, t=t, n_cls=_key_classes(t, ODD_CLASSES), lam_init=lam_init),
        grid=(n_batch, nqb),
        in_specs=[pl.BlockSpec(memory_space=pltpu.SMEM), qrow(1024),
                  pl.BlockSpec((None, 256, t), lambda b, i: (b, 0, 0)), pl.BlockSpec((t, 256), lambda b, i: (b, 0)),
                  _const_spec(near.shape), _const_spec(subg.shape)],
        out_specs=qrow(1024), out_shape=jax.ShapeDtypeStruct((n_batch * t, 1024), BF16),
        compiler_params=_params(("parallel", "arbitrary"), 40 * 2 ** 20),
        name="attn_odd_prompt",
    )(lam, q, ckt, cv, near, subg)


PAGES_PER_STEP = 32


def _page_specs(block, layer, n):
    nd = len(block)

    def spec(j):
        return pl.BlockSpec((None, None) + block,
                            lambda b, g, pt: (layer, pt[b, g * n + j]) + (0,) * nd)

    return [spec(j) for j in range(n)]


def _dec_scores_kernel(pt_ref, iq_ref, iw_ref, *refs, n):
    pages, out_ref = refs[:n], refs[n]
    iq = iq_ref[...]
    iw = iw_ref[...]
    for j in range(n):
        sc = jnp.maximum(_dot(iq, pages[j][...].astype(BF16)), 0.0) * iw
        out_ref[:, j * LANES:(j + 1) * LANES] = jnp.sum(sc, axis=0, keepdims=True)


def _dec_scores(page_table, iq_s, iw_s, cache_idx_t, layer):
    nb, npg = page_table.shape
    n = PAGES_PER_STEP if npg % PAGES_PER_STEP == 0 else 1
    page = cache_idx_t.shape[3]
    assert page == LANES
    grid_spec = pltpu.PrefetchScalarGridSpec(
        num_scalar_prefetch=1, grid=(nb, npg // n),
        in_specs=[pl.BlockSpec((None, 8, HD_IDX), lambda b, g, pt: (b, 0, 0)),
                  pl.BlockSpec((None, 8, 1), lambda b, g, pt: (b, 0, 0))]
        + _page_specs((HD_IDX, page), layer, n),
        out_specs=pl.BlockSpec((None, 1, n * page), lambda b, g, pt: (b, 0, g)))
    return pl.pallas_call(
        functools.partial(_dec_scores_kernel, n=n), grid_spec=grid_spec,
        out_shape=jax.ShapeDtypeStruct((nb, 1, npg * page), F32),
        compiler_params=_params(("parallel", "arbitrary"), 32 * 2 ** 20),
        name="decode_index_scores",
    )(page_table, iq_s, iw_s, *([cache_idx_t] * n))


def _dec_select_kernel(sc_ref, iq_ref, ik_ref, iw_ref, neg_ref, negself_ref, *, topk):
    rows, past = sc_ref.shape
    iq = iq_ref[...].astype(F32)
    ik = ik_ref[...]
    iw = iw_ref[...]
    self_score = jnp.zeros((rows, 1), F32)
    for hh in range(H_IDX):
        s = jnp.sum(iq[:, hh * HD_IDX:(hh + 1) * HD_IDX] * ik, axis=1, keepdims=True)
        self_score = self_score + jnp.maximum(s, 0.0) * iw[:, hh:hh + 1]
    lane = lax.broadcasted_iota(jnp.int32, (rows, LANES), 1)
    tail = jnp.where(lane == 0, self_score, NEG_INF)
    score = jnp.concatenate([sc_ref[...], tail], axis=1)
    col = lax.broadcasted_iota(jnp.int32, score.shape, 1)
    neg = jnp.where(col <= past, _topk_neg_mask(score, col, topk, 1), NEG_INF)
    neg_ref[...] = neg[:, :past]
    negself_ref[...] = neg[:, past:]


def _dec_select(scores, iq, ik, iw, topk):
    rows, past = scores.shape
    return pl.pallas_call(
        functools.partial(_dec_select_kernel, topk=topk),
        out_shape=[jax.ShapeDtypeStruct((rows, past), F32), jax.ShapeDtypeStruct((rows, LANES), F32)],
        compiler_params=pltpu.CompilerParams(vmem_limit_bytes=_vmem_limit(48 * 2 ** 20)),
        name="decode_topk_select",
    )(scores, iq, ik, iw)


def _flash_update(m_ref, s_ref, acc_ref, logits, pv_fn):
    m_old = m_ref[...]
    m_new = jnp.maximum(m_old, jnp.max(logits, axis=1, keepdims=True))
    m_safe = jnp.where(m_new == NEG_INF, 0.0, m_new)
    alpha = jnp.exp2(m_old - m_safe)
    p = jnp.exp2(logits - m_safe)
    s_ref[...] = alpha * s_ref[...] + jnp.sum(p, axis=1, keepdims=True)
    acc_ref[...] = alpha * acc_ref[...] + pv_fn(p.astype(BF16))
    m_ref[...] = m_new


def _flash_finish(m_ref, s_ref, acc_ref, l_self, v_self):
    m_old = m_ref[...]
    m_new = jnp.maximum(m_old, l_self)
    m_safe = jnp.where(m_new == NEG_INF, 0.0, m_new)
    alpha = jnp.exp2(m_old - m_safe)
    p = jnp.exp2(l_self - m_safe)
    return (alpha * acc_ref[...] + p * v_self) / (alpha * s_ref[...] + p)


def _add_near_last_page(logits, near, n):
    if n == 1:
        return logits + near
    return jnp.concatenate([logits[:, :(n - 1) * LANES], logits[:, (n - 1) * LANES:] + near], axis=1)


def _dec_even_kernel(pt_ref, neg_ref, negself_ref, aq_ref, qlat_ref, kvn_ref, latn_ref, near_ref, bself_ref,
                     *refs, n):
    kv_pages, lat_pages = refs[:n], refs[n:2 * n]
    oa_ref, ol_ref = refs[2 * n:2 * n + 2]
    ma_ref, sa_ref, acca_ref, mb_ref, sb_ref, accb_ref = refs[2 * n + 2:]
    g = pl.program_id(1)
    last = pl.num_programs(1) - 1

    @pl.when(g == 0)
    def _():
        ma_ref[...] = jnp.full(ma_ref.shape, NEG_INF, F32)
        mb_ref[...] = jnp.full(mb_ref.shape, NEG_INF, F32)
        sa_ref[...] = jnp.zeros(sa_ref.shape, F32)
        sb_ref[...] = jnp.zeros(sb_ref.shape, F32)
        acca_ref[...] = jnp.zeros(acca_ref.shape, F32)
        accb_ref[...] = jnp.zeros(accb_ref.shape, F32)

    aq = aq_ref[...]
    qlat = qlat_ref[...]
    kv = [p[...].astype(BF16) for p in kv_pages]
    la = jnp.concatenate([_dot(aq, kvp[:LANES]) for kvp in kv], axis=1) + neg_ref[...]
    la = _add_near_last_page(la, jnp.where(g == last, near_ref[...], 0.0), n)

    def pv_a(p):
        return sum(_dot_nt(p[:, j * LANES:(j + 1) * LANES], kv[j][LANES:]) for j in range(n))

    _flash_update(ma_ref, sa_ref, acca_ref, la, pv_a)
    lat = [p[...].astype(BF16) for p in lat_pages]
    zrows = jnp.zeros((256 - KV_LORA - ROPE_B, LANES), BF16)
    lb = jnp.concatenate([_dot(qlat, jnp.concatenate([lp, zrows], axis=0)) for lp in lat], axis=1)

    def pv_b(p):
        return sum(_dot_nt(p[:, j * LANES:(j + 1) * LANES], lat[j][:KV_LORA]) for j in range(n))

    _flash_update(mb_ref, sb_ref, accb_ref, lb, pv_b)

    @pl.when(g == last)
    def _():
        kvn = kvn_ref[...]
        l_self = (jnp.sum(aq.astype(F32) * kvn[:, :LANES], axis=1, keepdims=True) + bself_ref[...]
                  + negself_ref[:, 0:1])
        oa = _flash_finish(ma_ref, sa_ref, acca_ref, l_self, kvn[:, LANES:])
        rowi = lax.broadcasted_iota(jnp.int32, oa.shape, 0)
        oa = jnp.where(rowi < H_A // H_A_KV, oa, pltpu.roll(oa, HD_A, 1))
        oa_ref[...] = oa[:, :HD_A]
        latn = latn_ref[...]
        lb_self = jnp.sum(qlat.astype(F32) * latn, axis=1, keepdims=True)
        ol_ref[...] = _flash_finish(mb_ref, sb_ref, accb_ref, lb_self, latn[:, :KV_LORA]).astype(BF16)


def _dec_even(page_table, neg, negself, aq_s, qlat_s, kv_new, lat_new, near_s, bself, cache_kv_t, cache_lat_t,
              layer):
    nb, npg = page_table.shape
    n = PAGES_PER_STEP if npg % PAGES_PER_STEP == 0 else 1
    page = cache_kv_t.shape[3]
    per_seq = lambda shape: pl.BlockSpec((None,) + shape, lambda b, g, pt: (b,) + (0,) * len(shape))
    const = lambda shape: pl.BlockSpec(shape, lambda b, g, pt: (0,) * len(shape))
    grid_spec = pltpu.PrefetchScalarGridSpec(
        num_scalar_prefetch=1, grid=(nb, npg // n),
        in_specs=[pl.BlockSpec((None, 1, n * page), lambda b, g, pt: (b, 0, g)), per_seq((1, LANES)),
                  per_seq((H_A, LANES)), per_seq((H_B, 256)), per_seq((1, 256)), per_seq((1, 256)),
                  const((H_A, LANES)), const((H_A, 1))]
        + _page_specs((256, page), layer, n) + _page_specs((KV_LORA + ROPE_B, page), layer, n),
        out_specs=[per_seq((H_A, HD_A)), per_seq((H_B, KV_LORA))],
        scratch_shapes=[pltpu.VMEM((H_A, 1), F32), pltpu.VMEM((H_A, 1), F32), pltpu.VMEM((H_A, LANES), F32),
                        pltpu.VMEM((H_B, 1), F32), pltpu.VMEM((H_B, 1), F32), pltpu.VMEM((H_B, KV_LORA), F32)])
    return pl.pallas_call(
        functools.partial(_dec_even_kernel, n=n), grid_spec=grid_spec,
        out_shape=[jax.ShapeDtypeStruct((nb, H_A, HD_A), F32), jax.ShapeDtypeStruct((nb, H_B, KV_LORA), BF16)],
        compiler_params=_params(("parallel", "arbitrary"), 32 * 2 ** 20),
        name="decode_attn_even",
    )(page_table, neg, negself, aq_s, qlat_s, kv_new, lat_new, near_s, bself,
      *([cache_kv_t] * n), *([cache_lat_t] * n))


def _dec_odd_kernel(pt_ref, lam_ref, q_ref, kvn_ref, near_ref, bself_ref, subg_ref, *refs, n, lam_init):
    pages = refs[:n]
    o_ref = refs[n]
    m_ref, s_ref, acc_ref = refs[n + 1:]
    g = pl.program_id(1)
    last = pl.num_programs(1) - 1
    page = pages[0].shape[0] // (2 * H_C_KV)
    width = 2 * HD_C

    @pl.when(g == 0)
    def _():
        m_ref[...] = jnp.full(m_ref.shape, NEG_INF, F32)
        s_ref[...] = jnp.zeros(s_ref.shape, F32)
        acc_ref[...] = jnp.zeros(acc_ref.shape, F32)

    q = q_ref[...]
    row_g0 = lax.broadcasted_iota(jnp.int32, (2 * H_C, width), 0) % H_C < H_C // H_C_KV
    part = lambda p, r: p[pl.ds(r, page, stride=2 * H_C_KV), :].astype(BF16)
    logits = jnp.concatenate(
        [_dot_nt(q[:, :width], part(p, 0)) + _dot_nt(q[:, width:], part(p, 1)) for p in pages], axis=1)
    logits = _add_near_last_page(logits, jnp.where(g == last, near_ref[...], 0.0), n)

    def pv(p):
        v0 = sum(_dot(p[:, j * LANES:(j + 1) * LANES], part(pages[j], 2)) for j in range(n))
        v1 = sum(_dot(p[:, j * LANES:(j + 1) * LANES], part(pages[j], 3)) for j in range(n))
        return jnp.where(row_g0, v0, v1)

    _flash_update(m_ref, s_ref, acc_ref, logits, pv)

    @pl.when(g == last)
    def _():
        kvn = kvn_ref[...]
        l_self = jnp.sum(q.astype(F32) * kvn[:, :2 * width], axis=1, keepdims=True) + bself_ref[...]
        v_self = jnp.where(row_g0, kvn[:, 2 * width:3 * width], kvn[:, 3 * width:])
        o = _flash_finish(m_ref, s_ref, acc_ref, l_self, v_self)
        d = o[:H_C] - lam_ref[0] * o[H_C:]
        o_ref[...] = _rms(d, subg_ref[...]) * (1.0 - lam_init)


def _dec_odd(page_table, lam, q_s, kv_new, near_s, bself, subg, cache_kv_r, layer, lam_init):
    nb, npg = page_table.shape
    n = PAGES_PER_STEP if npg % PAGES_PER_STEP == 0 else 1
    rows, width = cache_kv_r.shape[2:]
    per_seq = lambda shape: pl.BlockSpec((None,) + shape, lambda b, g, pt: (b,) + (0,) * len(shape))
    const = lambda shape: pl.BlockSpec(shape, lambda b, g, pt: (0,) * len(shape))
    grid_spec = pltpu.PrefetchScalarGridSpec(
        num_scalar_prefetch=1, grid=(nb, npg // n),
        in_specs=[pl.BlockSpec(memory_space=pltpu.SMEM), per_seq((2 * H_C, H_C_KV * width)),
                  per_seq((1, 2 * H_C_KV * width)), const((2 * H_C, LANES)), const((2 * H_C, 1)), const((1, width))]
        + _page_specs((rows, width), layer, n),
        out_specs=per_seq((H_C, width)),
        scratch_shapes=[pltpu.VMEM((2 * H_C, 1), F32), pltpu.VMEM((2 * H_C, 1), F32),
                        pltpu.VMEM((2 * H_C, width), F32)])
    return pl.pallas_call(
        functools.partial(_dec_odd_kernel, n=n, lam_init=lam_init), grid_spec=grid_spec,
        out_shape=jax.ShapeDtypeStruct((nb, H_C, width), F32),
        compiler_params=_params(("parallel", "arbitrary"), 32 * 2 ** 20),
        name="decode_attn_odd",
    )(page_table, lam, q_s, kv_new, near_s, bself, subg, *([cache_kv_r] * n))


def _t5_bias(table, rel):
    n = jnp.maximum(rel, 0)
    max_exact = NUM_BUCKETS // 2
    nf = jnp.maximum(n, 1).astype(F32)
    large = max_exact + (jnp.log(nf / max_exact) / math.log(MAX_DISTANCE / max_exact)
                         * (NUM_BUCKETS - max_exact)).astype(jnp.int32)
    bucket = jnp.where(n < max_exact, n, jnp.minimum(large, NUM_BUCKETS - 1))
    return table[bucket].astype(F32)


def _far_distance_is_constant(first_far, max_rel):
    n = np.arange(first_far, max_rel + 1, dtype=np.float64)
    large = NUM_BUCKETS // 2 + np.floor(np.log(n / (NUM_BUCKETS // 2)) / math.log(MAX_DISTANCE / (NUM_BUCKETS // 2))
                                        * (NUM_BUCKETS - NUM_BUCKETS // 2) - 1e-3)
    return bool(np.all(large >= NUM_BUCKETS - 1))


def _near_tiles(table, per):
    r = jnp.arange(Q_BLOCK, dtype=jnp.int32)[:, None]
    c = jnp.arange(Q_BLOCK, dtype=jnp.int32)[None, :]
    far = table[NUM_BUCKETS - 1].astype(F32)
    before = jnp.transpose(_t5_bias(table, Q_BLOCK + r - c) - far, (2, 0, 1)) * LOG2E
    diag = jnp.transpose(_t5_bias(table, r - c) - far, (2, 0, 1)) * LOG2E
    z = jnp.zeros((table.shape[1], per - 1, Q_BLOCK, Q_BLOCK), F32)
    return jnp.concatenate([z, before[:, None], diag[:, None], z], axis=1)


def _near_rows(table, page):
    rel = page - jnp.arange(page, dtype=jnp.int32)
    far = table[NUM_BUCKETS - 1].astype(F32)
    return ((_t5_bias(table, rel) - far) * LOG2E).T, ((_t5_bias(table, jnp.zeros((1,), jnp.int32)) - far) * LOG2E).T


def _rope_tables(pos, rows):
    half = ROPE_B // 2
    inv = ROPE_THETA ** (-jnp.arange(half, dtype=F32) / half)
    ang = pos.astype(F32)[:, None] * inv[None, :]
    cos, sin = jnp.cos(ang), jnp.sin(ang)
    z = jnp.zeros_like(cos)
    pad = jnp.zeros((pos.shape[0], LANES - ROPE_B), F32)
    tabs = [jnp.concatenate([cos, cos, pad], 1), jnp.concatenate([sin, sin, pad], 1),
            jnp.concatenate([-sin, z, pad], 1), jnp.concatenate([z, sin, pad], 1)]
    return [jnp.broadcast_to(t, (rows, LANES)) if t.shape[0] != rows else t for t in tabs]


def _prep_ab_weights(w_in, w_uq, w_uk, w_uv):
    d = w_in.shape[0]
    o = np.cumsum([0, H_A * HD_A, H_A_KV * HD_A, H_A_KV * HD_A, H_IDX * HD_IDX, HD_IDX, H_IDX, Q_LORA, KV_LORA,
                   ROPE_B])
    sec = lambda i: w_in[:, o[i]:o[i + 1]]
    w1 = jnp.concatenate([sec(0), sec(1), sec(2), sec(3), sec(6), sec(7), sec(8), sec(5),
                          jnp.zeros((d, GRP_IK - GRP_IW - H_IDX), w_in.dtype), sec(4)], axis=1).astype(BF16)
    assert w1.shape[1] == AB_COLS
    uq = w_uq.reshape(Q_LORA, H_B, NOPE_B + ROPE_B)
    nope = uq[:, :, :NOPE_B].reshape(Q_LORA, H_B * NOPE_B)
    r = uq[:, :, NOPE_B:]
    half = ROPE_B // 2
    rot = jnp.concatenate([-r[:, :, half:], r[:, :, :half]], axis=-1)
    padl = lambda a: jnp.pad(a, ((0, 0), (0, 0), (0, LANES - ROPE_B))).reshape(Q_LORA, H_B * LANES)
    w2 = jnp.concatenate([nope, padl(r), padl(rot)], axis=1).astype(BF16)
    eye = jnp.eye(H_B, dtype=w_uk.dtype)
    wuk = jnp.einsum("chn,hg->hngc", w_uk, eye).reshape(H_B * NOPE_B, H_B * KV_LORA).astype(BF16)
    wv = jnp.transpose(w_uv, (1, 0, 2))
    zl = jnp.zeros_like(wv)
    even = (jnp.arange(H_B) % 2 == 0)[:, None, None]
    wuv = jnp.where(even, jnp.concatenate([wv, zl], -1), jnp.concatenate([zl, wv], -1)).astype(BF16)
    wuv_bd = jnp.einsum("chv,hg->hcgv", w_uv, eye).reshape(H_B * KV_LORA, H_B * V_B).astype(BF16)
    return w1, w2, wuk, wuv, wuv_bd


def _lambda(lambda_qk, layer):
    lam_init = 0.8 - 0.6 * math.exp(-0.3 * layer)
    lq = lambda_qk.astype(F32)
    lam = jnp.exp(jnp.sum(lq[0] * lq[1])) - jnp.exp(jnp.sum(lq[2] * lq[3])) + lam_init
    return lam.reshape(1), lam_init


def _run_prompt(x, P):
    nb, t, d = x.shape
    assert t % Q_BLOCK == 0
    m = nb * t
    topk = min(TOPK_MAX, t // 4)
    xf = x.reshape(m, d)
    tabs = _rope_tables(jnp.arange(t, dtype=jnp.int32), t)
    a_kv_l, a_idx_l, b_lat_l, c_kv_l = [], [], [], []
    depth = P["attn_norm"].shape[0]
    for l in range(depth):
        last = l == depth - 1
        if l % 2 == 0:
            e = l // 2
            w1, w2, wuk, wuv, _ = P["ab"][e]
            (aq, akv, iq, ik, iw, qlat, blat, av2, latc, ik2, akt, iwt, latt) = _ab_proj(
                xf, P["attn_norm"][l][None], w1, P["g_q"][e][None], w2, wuk, P["g_kv"][e][None], tabs, nb, t, True)
            o = _attn_even(aq, iq, iwt, qlat, akt, av2, ik2, latt, latc, P["near_a"], wuv, nb, t, topk)
            wo = P["w_o_ab"][e]
            a_kv_l.append(akv.reshape(nb, t, 2, H_A_KV, HD_A))
            a_idx_l.append(ik.reshape(nb, t, HD_IDX))
            b_lat_l.append(blat.reshape(nb, t, KV_LORA + ROPE_B))
        else:
            j = l // 2
            lam, lam_init = _lambda(P["lambda_qk"][j], l)
            q, ckv, ckt, cv = _c_proj(xf, P["attn_norm"][l][None], P["w_in_c"][j], nb, t, True)
            o = _attn_odd(lam, q, ckt, cv, P["near_c"], P["sub_norm_c"][j][None], nb, t, lam_init)
            wo = P["w_o_c"][j]
            c_kv_l.append(ckv.reshape(nb, t, 2, H_C_KV, 2 * HD_C))
        xf = _out_ffn(xf, o, wo, P["ffn_norm"][l][None], P["w_gate_up"][l], P["w_down"][l], P["final_norm"][None],
                      last)
    return (xf.reshape(nb, t, d), jnp.stack(a_kv_l), jnp.stack(a_idx_l), jnp.stack(b_lat_l), jnp.stack(c_kv_l))


def _run_sample(x, P, caches, page_table):
    cache_a_kv, cache_a_idx, cache_b_latent, cache_c_kv = caches
    nb, t, d = x.shape
    assert t == 1
    npg = page_table.shape[1]
    page = cache_a_idx.shape[2]
    past = npg * page
    topk = min(TOPK_MAX, (past + t) // 4)
    assert _far_distance_is_constant(page + 1, past)
    xf = x.reshape(nb, d)
    tabs = _rope_tables(jnp.full((1,), past, jnp.int32), nb)
    idx_t = jnp.transpose(cache_a_idx, (0, 1, 3, 2))
    kv_t = jnp.transpose(cache_a_kv, (0, 1, 3, 4, 5, 2)).reshape(cache_a_kv.shape[:2] + (-1, page))
    lat_t = jnp.transpose(cache_b_latent, (0, 1, 3, 2))
    ckv_r = cache_c_kv.reshape(cache_c_kv.shape[:2] + (-1, cache_c_kv.shape[-1]))
    a_kv_l, a_idx_l, b_lat_l, c_kv_l = [], [], [], []
    depth = P["attn_norm"].shape[0]
    gsz_a = H_A // H_A_KV
    head_lo_a = (jnp.arange(H_A) < gsz_a)[None, :, None]
    for l in range(depth):
        last = l == depth - 1
        if l % 2 == 0:
            e = l // 2
            w1, w2, wuk, _, wuv_bd = P["ab"][e]
            aq, akv, iq, ik, iw, qlat, blat = _ab_proj(
                xf, P["attn_norm"][l][None], w1, P["g_q"][e][None], w2, wuk, P["g_kv"][e][None], tabs, nb, t, False)
            iq_s = jnp.pad(iq.reshape(nb, H_IDX, HD_IDX), ((0, 0), (0, 8 - H_IDX), (0, 0)))
            iw_s = jnp.pad(iw, ((0, 0), (0, 8 - H_IDX)))[:, :, None]
            scores = _dec_scores(page_table, iq_s, iw_s, idx_t, e)
            neg, negself = _dec_select(scores.reshape(nb, past), iq, ik, iw, topk)
            aq3 = aq.reshape(nb, H_A, HD_A)
            zq = jnp.zeros_like(aq3)
            aq_s = jnp.where(head_lo_a, jnp.concatenate([aq3, zq], -1), jnp.concatenate([zq, aq3], -1))
            lat_new = jnp.pad(blat, ((0, 0), (0, 256 - blat.shape[1])))
            oa, olat = _dec_even(page_table, neg.reshape(nb, 1, past), negself.reshape(nb, 1, LANES), aq_s,
                                 qlat.reshape(nb, H_B, 256), akv.reshape(nb, 1, 256), lat_new.reshape(nb, 1, 256),
                                 P["near_a_s"], P["bself_a"], kv_t, lat_t, e)
            ob = _mm(olat.reshape(nb, H_B * KV_LORA), wuv_bd, BF16)
            o = jnp.concatenate([oa.reshape(nb, H_A * HD_A).astype(BF16), ob], axis=1)
            wo = P["w_o_ab"][e]
            a_kv_l.append(akv.reshape(nb, t, 2, H_A_KV, HD_A))
            a_idx_l.append(ik.reshape(nb, t, HD_IDX))
            b_lat_l.append(blat.reshape(nb, t, KV_LORA + ROPE_B))
        else:
            j = l // 2
            lam, lam_init = _lambda(P["lambda_qk"][j], l)
            q, ckv = _c_proj(xf, P["attn_norm"][l][None], P["w_in_c"][j], nb, t, False)
            q4 = q.reshape(nb, H_C, 2, HD_C)
            sel = ((jnp.arange(H_C)[:, None, None, None] // (H_C // H_C_KV) == jnp.arange(H_C_KV)[None, None, :, None])
                   & (jnp.arange(2)[None, :, None, None] == jnp.arange(2)[None, None, None, :]))
            q_s = jnp.where(sel[None, :, :, :, :, None], q4[:, :, :, None, None, :], jnp.zeros((), q.dtype))
            q_s = jnp.transpose(q_s, (0, 2, 1, 3, 4, 5)).reshape(nb, 2 * H_C, H_C_KV * 2 * HD_C)
            oc = _dec_odd(page_table, lam, q_s, ckv.reshape(nb, 1, -1), P["near_c_s"], P["bself_c"],
                          P["sub_norm_c"][j][None], ckv_r, j, lam_init)
            o = oc.reshape(nb, H_C * 2 * HD_C).astype(BF16)
            wo = P["w_o_c"][j]
            c_kv_l.append(ckv.reshape(nb, t, 2, H_C_KV, 2 * HD_C))
        xf = _out_ffn(xf, o, wo, P["ffn_norm"][l][None], P["w_gate_up"][l], P["w_down"][l], P["final_norm"][None],
                      last)
    return (xf.reshape(nb, t, d), jnp.stack(a_kv_l), jnp.stack(a_idx_l), jnp.stack(b_lat_l), jnp.stack(c_kv_l))


def kernel(x_prompt, x_sample, cache_a_kv, cache_a_idx, cache_b_latent, cache_c_kv, page_table, attn_norm, w_in_ab,
           w_uq, g_q, g_kv, w_uk, w_uv, w_o_ab, rel_bias, w_in_c, lambda_qk, sub_norm_c, w_o_c, ffn_norm, w_gate_up,
           w_down, final_norm):
    page = cache_a_idx.shape[2]
    assert page == LANES
    assert _far_distance_is_constant(Q_BLOCK + 1, max(x_prompt.shape[1], Q_BLOCK + 1))
    tab_a, tab_c = rel_bias[:, :H_A], rel_bias[:, H_A:]
    nqb = x_prompt.shape[1] // Q_BLOCK
    per_a = nqb // _key_classes(x_prompt.shape[1], EVEN_CLASSES)
    per_c = nqb // _key_classes(x_prompt.shape[1], ODD_CLASSES)
    near_a_s, bself_a = _near_rows(tab_a, page)
    near_c_s, bself_c = _near_rows(tab_c, page)
    P = dict(
        attn_norm=attn_norm, g_q=g_q, g_kv=g_kv, ffn_norm=ffn_norm, final_norm=final_norm, lambda_qk=lambda_qk,
        sub_norm_c=sub_norm_c,
        ab=[_prep_ab_weights(w_in_ab[e], w_uq[e], w_uk[e], w_uv[e]) for e in range(w_in_ab.shape[0])],
        w_o_ab=w_o_ab.astype(BF16), w_in_c=w_in_c.astype(BF16), w_o_c=w_o_c.astype(BF16),
        w_gate_up=w_gate_up.astype(BF16), w_down=w_down.astype(BF16),
        near_a=_near_tiles(tab_a, per_a), near_c=_near_tiles(tab_c, per_c),
        near_a_s=near_a_s, bself_a=bself_a,
        near_c_s=jnp.concatenate([near_c_s, near_c_s], 0), bself_c=jnp.concatenate([bself_c, bself_c], 0),
    )
    y_p, akv_p, aidx_p, blat_p, ckv_p = _run_prompt(x_prompt, P)
    y_s, akv_s, aidx_s, blat_s, ckv_s = _run_sample(
        x_sample, P, (cache_a_kv, cache_a_idx, cache_b_latent, cache_c_kv), page_table)
    return (y_p, y_s, akv_p, aidx_p, blat_p, ckv_p, akv_s, aidx_s, blat_s, ckv_s)
```

```python
import functools
import math

import numpy as np
import jax
import jax.numpy as jnp
from jax import lax
from jax.experimental import pallas as pl
from jax.experimental.pallas import tpu as pltpu

EPS = 1e-6
NUM_BUCKETS = 32
MAX_DISTANCE = 128
H_A, HD_A, H_A_KV = 8, 64, 2
H_IDX, HD_IDX = 4, 64
TOPK_MAX = 256
H_B, NOPE_B, ROPE_B, V_B = 8, 64, 32, 64
Q_LORA, KV_LORA = 256, 128
ROPE_THETA = 10000.0
MLA_SCALE = (NOPE_B + ROPE_B) ** -0.5
H_C, HD_C, H_C_KV = 8, 64, 2

LANES = 128
V7X_VMEM_BYTES = 64 * 2 ** 20

Q_BLOCK = LANES
F32 = jnp.float32
BF16 = jnp.bfloat16
NEG_INF = float("-inf")
SUM_ROWS = 64
LOG2E = 1.4426950408889634
INT_MIN = -2 ** 31


def _vmem_limit(nbytes):
    return int(min(max(nbytes, 32 * 2 ** 20), V7X_VMEM_BYTES - 8 * 2 ** 20))


def _params(semantics, vmem_bytes):
    return pltpu.CompilerParams(dimension_semantics=semantics, vmem_limit_bytes=_vmem_limit(vmem_bytes))


def _rms(x, g):
    return x * lax.rsqrt(jnp.mean(x * x, axis=-1, keepdims=True) + EPS) * g


def _dot(a, b):
    return jnp.dot(a, b, preferred_element_type=F32)


def _dot_nt(a, b):
    return lax.dot_general(a, b, (((1,), (1,)), ((), ())), preferred_element_type=F32)


def _const_spec(shape):
    nd = len(shape)
    return pl.BlockSpec(shape, lambda *_: (0,) * nd, pipeline_mode=pl.Buffered(1))


def _ordered_bits_to_float(u):
    key = jnp.maximum(u ^ jnp.int32(INT_MIN), jnp.int32(-0x7F800001))
    bits = jnp.where(key >= 0, key, key ^ jnp.int32(0x7FFFFFFF))
    return lax.bitcast_convert_type(bits, F32)


def _topk_neg_mask(score, idx, k, axis):
    n = score.shape[axis]
    state = (score.shape[0], 1) if axis == 1 else (1, score.shape[1])
    kf = jnp.float32(k)

    def reduce_sum(x):
        if axis == 0 and n % SUM_ROWS == 0 and n > SUM_ROWS:
            x = jnp.sum(x.reshape(n // SUM_ROWS, SUM_ROWS, x.shape[1]), axis=0)
        return jnp.sum(x, axis=axis, keepdims=True)

    def count(mask):
        return reduce_sum(jnp.where(mask, 1.0, 0.0))

    def value_step(it, u):
        c = u | lax.shift_left(jnp.int32(1), 31 - it)
        return jnp.where(count(score >= _ordered_bits_to_float(c)) >= kf, c, u)

    u = lax.fori_loop(0, 32, value_step, jnp.zeros(state, jnp.int32))
    thr = _ordered_bits_to_float(u)
    gt = score > thr
    eq = score == thr
    need = kf - count(gt)
    n_eq = count(eq)
    nbits = int(n).bit_length()

    def tie_search():
        eqf = jnp.where(eq, 1.0, 0.0)

        def idx_step(it, j):
            c = j | lax.shift_left(jnp.int32(1), nbits - 1 - it)
            m = reduce_sum(jnp.where(idx < c, eqf, 0.0))
            return jnp.where(m <= need, c, j)

        return lax.fori_loop(0, nbits, idx_step, jnp.zeros(state, jnp.int32))

    overflow = jnp.max(n_eq - need) > 0.0
    jcut = lax.cond(overflow, tie_search, lambda: jnp.full(state, 2 ** nbits - 1, jnp.int32))
    keep_tie = jnp.where(idx < jcut, 0.0, NEG_INF)
    return jnp.where(gt, 0.0, jnp.where(eq, keep_tie, NEG_INF))


AB_AQ, AB_AK, AB_AV, AB_IQ, AB_CQ, AB_CKV, AB_GRP, AB_COLS = 0, 512, 640, 768, 1024, 1280, 1408, 1536
GRP_KR, GRP_IW, GRP_IK = 0, 32, 64


def _ab_proj_kernel(x_ref, g_ref, w1_ref, gq_ref, w2_ref, wuk_ref, gkv_ref, cos_ref, sin_ref, nsl_ref, shi_ref,
                    aq_ref, akv_ref, iq_ref, ik_ref, iw_ref, qlat_ref, blat_ref, *t_refs, emit_t):
    h = _rms(x_ref[...], g_ref[...]).astype(BF16)
    y = _dot(h, w1_ref[...])
    aq_ref[...] = (y[:, AB_AQ:AB_AK] * (HD_A ** -0.5 * LOG2E)).astype(BF16)
    akv_ref[...] = y[:, AB_AK:AB_IQ]
    iq_ref[...] = (y[:, AB_IQ:AB_CQ] * HD_IDX ** -0.5).astype(BF16)
    grp = y[:, AB_GRP:AB_COLS]
    ik_ref[...] = grp[:, GRP_IK:GRP_IK + HD_IDX]
    iw_ref[...] = grp[:, GRP_IW:GRP_IW + H_IDX] * H_IDX ** -0.5
    cq = _rms(y[:, AB_CQ:AB_CKV], gq_ref[...]).astype(BF16)
    q2 = _dot(cq, w2_ref[...])
    n_nope = H_B * NOPE_B
    n_pad = H_B * LANES
    qc = _dot(q2[:, :n_nope].astype(BF16), wuk_ref[...])
    cos8 = jnp.concatenate([cos_ref[...]] * H_B, axis=1)
    sin8 = jnp.concatenate([sin_ref[...]] * H_B, axis=1)
    qr = q2[:, n_nope:n_nope + n_pad] * cos8 + q2[:, n_nope + n_pad:] * sin8
    pieces = []
    for hh in range(H_B):
        pieces += [qc[:, hh * LANES:(hh + 1) * LANES], qr[:, hh * LANES:(hh + 1) * LANES]]
    qlat_ref[...] = (jnp.concatenate(pieces, axis=1) * (MLA_SCALE * LOG2E)).astype(BF16)
    latc = _rms(y[:, AB_CKV:AB_GRP], gkv_ref[...])
    half = ROPE_B // 2
    kr = (grp * cos_ref[...] + pltpu.roll(grp, LANES - half, 1) * nsl_ref[...]
          + pltpu.roll(grp, half, 1) * shi_ref[...])
    blat_ref[:, :KV_LORA] = latc
    blat_ref[:, KV_LORA:] = kr[:, :ROPE_B]
    if emit_t:
        av2_ref, latc_ref, ik2_ref, akt_ref, iwt_ref, latt_ref = t_refs
        lane = lax.broadcasted_iota(jnp.int32, (y.shape[0], LANES), 1)
        v = y[:, AB_AV:AB_IQ]
        vs = pltpu.roll(v, HD_A, 1)
        av2_ref[...] = jnp.concatenate([jnp.where(lane < HD_A, v, vs), jnp.where(lane < HD_A, vs, v)],
                                       axis=1).astype(BF16)
        latc_ref[...] = latc.astype(BF16)
        kt = y[:, AB_AK:AB_AV].T
        akt_ref[...] = jnp.concatenate([kt[:HD_A], kt[:HD_A], kt[HD_A:], kt[HD_A:]], axis=0).astype(BF16)
        ik2_ref[...] = jnp.where(lane < HD_IDX, pltpu.roll(grp, HD_IDX, 1), grp).astype(BF16)
        iwt_ref[...] = grp.T[GRP_IW:GRP_IW + 8] * H_IDX ** -0.5
        latt_ref[...] = jnp.concatenate([latc.T, kr.T], axis=0).astype(BF16)


def _ab_proj(x, g, w1, gq, w2, wuk, gkv, rope_tabs, n_batch, t, emit_t):
    m = x.shape[0]
    tm = min(512, t) if emit_t else m
    assert m % tm == 0 and (not emit_t or t % tm == 0)
    tpb = max(t // tm, 1)
    row = lambda cols: pl.BlockSpec((tm, cols), lambda i: (i, 0))
    tab = pl.BlockSpec((tm, LANES), (lambda i: (i % tpb, 0)) if emit_t else (lambda i: (0, 0)))
    tsp = lambda rows: pl.BlockSpec((None, rows, tm), lambda i: (i // tpb, 0, i % tpb))
    out_shape = [jax.ShapeDtypeStruct((m, 512), BF16), jax.ShapeDtypeStruct((m, 256), F32),
                 jax.ShapeDtypeStruct((m, 256), BF16), jax.ShapeDtypeStruct((m, HD_IDX), F32),
                 jax.ShapeDtypeStruct((m, H_IDX), F32), jax.ShapeDtypeStruct((m, H_B * 256), BF16),
                 jax.ShapeDtypeStruct((m, KV_LORA + ROPE_B), F32)]
    out_specs = [row(512), row(256), row(256), row(HD_IDX), row(H_IDX), row(H_B * 256), row(KV_LORA + ROPE_B)]
    if emit_t:
        out_shape += [jax.ShapeDtypeStruct((m, 256), BF16), jax.ShapeDtypeStruct((m, KV_LORA), BF16),
                      jax.ShapeDtypeStruct((m, 2 * HD_IDX), BF16),
                      jax.ShapeDtypeStruct((n_batch, 256, t), BF16), jax.ShapeDtypeStruct((n_batch, 8, t), F32),
                      jax.ShapeDtypeStruct((n_batch, 256, t), BF16)]
        out_specs += [row(256), row(KV_LORA), row(2 * HD_IDX), tsp(256), tsp(8), tsp(256)]
    return pl.pallas_call(
        functools.partial(_ab_proj_kernel, emit_t=emit_t),
        grid=(m // tm,),
        in_specs=[row(x.shape[1]), _const_spec(g.shape), _const_spec(w1.shape), _const_spec(gq.shape),
                  _const_spec(w2.shape), _const_spec(wuk.shape), _const_spec(gkv.shape), tab, tab, tab, tab],
        out_specs=out_specs, out_shape=out_shape,
        compiler_params=_params(("parallel",), 40 * 2 ** 20),
        name="ab_proj_prompt" if emit_t else "ab_proj_sample",
    )(x, g, w1, gq, w2, wuk, gkv, *rope_tabs)


def _c_proj_kernel(x_ref, g_ref, w_ref, q_ref, ckv_ref, *t_refs, emit_t):
    h = _rms(x_ref[...], g_ref[...]).astype(BF16)
    y = _dot(h, w_ref[...])
    nq = H_C * 2 * HD_C
    nk = H_C_KV * 2 * HD_C
    q_ref[...] = (y[:, :nq] * (HD_C ** -0.5 * LOG2E)).astype(BF16)
    if not emit_t:
        ckv_ref[...] = y[:, nq:]
    if emit_t:
        width = 2 * HD_C
        for j in range(2 * H_C_KV):
            ckv_ref[pl.ds(j, y.shape[0], stride=2 * H_C_KV), :] = y[:, nq + j * width:nq + (j + 1) * width]
        ckt_ref, cv_ref = t_refs
        ckt_ref[...] = y[:, nq:nq + nk].T.astype(BF16)
        cv_ref[...] = y[:, nq + nk:].astype(BF16)


def _c_proj(x, g, w, n_batch, t, emit_t):
    m = x.shape[0]
    tm = min(512, t) if emit_t else m
    tpb = max(t // tm, 1)
    nq, nk = H_C * 2 * HD_C, H_C_KV * 2 * HD_C
    row = lambda cols: pl.BlockSpec((tm, cols), lambda i: (i, 0))
    out_shape = [jax.ShapeDtypeStruct((m, nq), BF16), jax.ShapeDtypeStruct((m, 2 * nk), F32)]
    out_specs = [row(nq), row(2 * nk)]
    if emit_t:
        rows = 2 * H_C_KV
        out_shape[1] = jax.ShapeDtypeStruct((m * rows, 2 * HD_C), F32)
        out_specs[1] = pl.BlockSpec((tm * rows, 2 * HD_C), lambda i: (i, 0))
        out_shape += [jax.ShapeDtypeStruct((n_batch, nk, t), BF16), jax.ShapeDtypeStruct((m, nk), BF16)]
        out_specs += [pl.BlockSpec((None, nk, tm), lambda i: (i // tpb, 0, i % tpb)), row(nk)]
    return pl.pallas_call(
        functools.partial(_c_proj_kernel, emit_t=emit_t),
        grid=(m // tm,),
        in_specs=[row(x.shape[1]), _const_spec(g.shape), _const_spec(w.shape)],
        out_specs=out_specs, out_shape=out_shape,
        compiler_params=_params(("parallel",), 32 * 2 ** 20),
        name="c_proj_prompt" if emit_t else "c_proj_sample",
    )(x, g, w)


def _out_ffn_kernel(x_ref, o_ref, wo_ref, g_ref, wgu_ref, wd_ref, fg_ref, out_ref, *, d_ff, chunk, final):
    x1 = x_ref[...] + _dot(o_ref[...], wo_ref[...])
    h = _rms(x1, g_ref[...]).astype(BF16)
    acc = x1
    for c in range(d_ff // chunk):
        gate = _dot(h, wgu_ref[:, c * chunk:(c + 1) * chunk])
        up = _dot(h, wgu_ref[:, d_ff + c * chunk:d_ff + (c + 1) * chunk])
        act = gate * jax.nn.sigmoid(gate) * up
        acc = acc + _dot(act.astype(BF16), wd_ref[c * chunk:(c + 1) * chunk, :])
    if final:
        acc = _rms(acc, fg_ref[...])
    out_ref[...] = acc


def _out_ffn(x, o, wo, g, wgu, wd, fg, final):
    m, d = x.shape
    d_ff = wd.shape[0]
    chunk = 256 if d_ff % 256 == 0 else LANES
    assert d_ff % chunk == 0
    tm = min(512, m)
    assert m % tm == 0
    row = lambda cols: pl.BlockSpec((tm, cols), lambda i: (i, 0))
    return pl.pallas_call(
        functools.partial(_out_ffn_kernel, d_ff=d_ff, chunk=chunk, final=final),
        grid=(m // tm,),
        in_specs=[row(d), row(o.shape[1]), _const_spec(wo.shape), _const_spec(g.shape), _const_spec(wgu.shape),
                  _const_spec(wd.shape), _const_spec(fg.shape)],
        out_specs=row(d), out_shape=jax.ShapeDtypeStruct((m, d), F32),
        compiler_params=_params(("parallel",), 52 * 2 ** 20),
        name="out_ffn",
    )(x, o, wo, g, wgu, wd, fg)


def _mm_kernel(a_ref, b_ref, o_ref):
    o_ref[...] = _dot(a_ref[...], b_ref[...]).astype(o_ref.dtype)


def _mm(a, b, out_dtype):
    return pl.pallas_call(_mm_kernel, out_shape=jax.ShapeDtypeStruct((a.shape[0], b.shape[1]), out_dtype),
                          name="small_matmul")(a, b)


def _softmax_parts(logits):
    m = jnp.max(logits, axis=1, keepdims=True)
    p = jnp.exp2(logits - m)
    return p.astype(BF16), 1.0 / jnp.sum(p, axis=1, keepdims=True)


def _add_to_blocks(logits, addends):
    nb = logits.shape[1] // Q_BLOCK
    cols = [logits[:, j * Q_BLOCK:(j + 1) * Q_BLOCK] for j in range(nb)]
    return jnp.concatenate([cols[j] + addends[j] if j in addends else cols[j] for j in range(nb)], axis=1)


def _pipelined(units, ahead=2):
    pending = [units[i][0]() for i in range(min(ahead, len(units)))]
    for i, (_, finish) in enumerate(units):
        if i + ahead < len(units):
            pending.append(units[i + ahead][0]())
        finish(pending.pop(0))


def _near_addends(near_ref, head, r, first, per):
    return {first - 1 + jj: near_ref[head, per - 1 - r + jj] for jj in range(per + 1) if first - 1 + jj >= 0}


def _diag_causal(qb, first, per):
    w = per * Q_BLOCK
    row = lax.broadcasted_iota(jnp.int32, (Q_BLOCK, w), 0)
    col = lax.broadcasted_iota(jnp.int32, (Q_BLOCK, w), 1)
    return jnp.where(first * Q_BLOCK + col <= qb * Q_BLOCK + row, 0.0, NEG_INF)


def _attn_even_body(cls, per, qb, topk, aq_ref, iq_ref, iwt_ref, qlat_ref, akt_ref, av2_ref, ik2_ref, latt_ref,
                    latc_ref, near_ref, wuv_ref, o_ref):
    first = cls * per
    s = (cls + 1) * per * Q_BLOCK
    r = qb - first
    lo = lax.broadcasted_iota(jnp.int32, (Q_BLOCK, LANES), 1) < HD_A
    zero = jnp.zeros((), BF16)
    key = lax.broadcasted_iota(jnp.int32, (s, Q_BLOCK), 0)
    adm_t = key <= qb * Q_BLOCK + lax.broadcasted_iota(jnp.int32, (s, Q_BLOCK), 1)
    iq = iq_ref[...]
    iwt = iwt_ref[...]
    ik2 = ik2_ref[:s, :]
    score_t = jnp.zeros((s, Q_BLOCK), F32)
    for hp in range(H_IDX // 2):
        grp = iq[:, hp * LANES:(hp + 1) * LANES]
        q2 = jnp.concatenate([jnp.where(lo, grp, zero), jnp.where(lo, zero, grp)], axis=0)
        sc = jnp.maximum(_dot_nt(ik2, q2), 0.0)
        score_t = (score_t + sc[:, :Q_BLOCK] * iwt[2 * hp:2 * hp + 1, :]
                   + sc[:, Q_BLOCK:] * iwt[2 * hp + 1:2 * hp + 2, :])
    score_t = jnp.where(adm_t, score_t, NEG_INF)
    neg_sel = jnp.where(adm_t, _topk_neg_mask(score_t, key, topk, 0), NEG_INF).T
    aq = aq_ref[...]
    gsz = H_A // H_A_KV
    diag = _diag_causal(qb, first, per)
    causal = {first + j: diag[:, j * Q_BLOCK:(j + 1) * Q_BLOCK] for j in range(per)}
    base = H_A * HD_A
    held = {}

    def issue_a(head):
        g, pair, e = head // gsz, (head % gsz) // 2, head % 2
        grp = aq[:, (g * gsz // 2 + pair) * LANES:(g * gsz // 2 + pair + 1) * LANES]
        qm = jnp.where(lo, grp, zero) if e == 0 else jnp.where(lo, zero, grp)
        logits = _dot(qm, akt_ref[g * LANES:(g + 1) * LANES, :s]) + neg_sel
        return _add_to_blocks(logits, _near_addends(near_ref, head, r, first, per))

    def finish_a(head, logits):
        g = head // gsz
        p, inv = _softmax_parts(logits)
        o = _dot(p, av2_ref[:s, g * LANES:(g + 1) * LANES]) * inv
        if head % 2 == 0:
            held["a"] = o
        else:
            o_ref[:, (head // 2) * LANES:(head // 2 + 1) * LANES] = jnp.where(lo, held.pop("a"), o).astype(BF16)

    def issue_b(head):
        return _add_to_blocks(_dot(qlat_ref[:, head * 256:(head + 1) * 256], latt_ref[:, :s]), causal)

    def finish_b(head, logits):
        p, inv = _softmax_parts(logits)
        olat = (_dot(p, latc_ref[:s, :]) * inv).astype(BF16)
        part = _dot(olat, wuv_ref[head])
        if head % 2 == 0:
            held["b"] = part
        else:
            col0 = base + (head // 2) * LANES
            o_ref[:, col0:col0 + LANES] = (held.pop("b") + part).astype(BF16)

    _pipelined([(functools.partial(issue_a, h), functools.partial(finish_a, h)) for h in range(H_A)]
               + [(functools.partial(issue_b, h), functools.partial(finish_b, h)) for h in range(H_B)])


def _attn_even_kernel(aq_ref, iq_ref, iwt_ref, qlat_ref, akt_ref, av2_ref, ik2_ref, latt_ref, latc_ref, near_ref,
                      wuv_ref, o_ref, *, t, n_cls, topk):
    qb = pl.program_id(1)
    per = (t // Q_BLOCK) // n_cls
    for c in range(n_cls):
        @pl.when(qb // per == c)
        def _(c=c):
            _attn_even_body(c, per, qb, topk, aq_ref, iq_ref, iwt_ref, qlat_ref, akt_ref, av2_ref, ik2_ref,
                            latt_ref, latc_ref, near_ref, wuv_ref, o_ref)


EVEN_CLASSES = 4
ODD_CLASSES = 8


def _key_classes(t, want):
    nqb = t // Q_BLOCK
    while nqb % want:
        want //= 2
    return want


def _attn_even(aq, iq, iwt, qlat, akt, av2, ik2, latt, latc, near, wuv, n_batch, t, topk):
    nqb = t // Q_BLOCK
    qrow = lambda cols: pl.BlockSpec((Q_BLOCK, cols), lambda b, i: (b * nqb + i, 0))
    krow = lambda cols: pl.BlockSpec((t, cols), lambda b, i: (b, 0))
    kcol = lambda rows: pl.BlockSpec((None, rows, t), lambda b, i: (b, 0, 0))
    return pl.pallas_call(
        functools.partial(_attn_even_kernel, t=t, n_cls=_key_classes(t, EVEN_CLASSES), topk=topk),
        grid=(n_batch, nqb),
        in_specs=[qrow(512), qrow(256), pl.BlockSpec((None, 8, Q_BLOCK), lambda b, i: (b, 0, i)), qrow(H_B * 256),
                  kcol(256), krow(256), krow(2 * HD_IDX), kcol(256), krow(KV_LORA), _const_spec(near.shape),
                  _const_spec(wuv.shape)],
        out_specs=qrow(1024), out_shape=jax.ShapeDtypeStruct((n_batch * t, 1024), BF16),
        compiler_params=_params(("parallel", "arbitrary"), 48 * 2 ** 20),
        name="attn_even_prompt",
    )(aq, iq, iwt, qlat, akt, av2, ik2, latt, latc, near, wuv)


def _attn_odd_body(cls, per, qb, lam, lam_init, q_ref, ckt_ref, cv_ref, near_ref, subg_ref, o_ref):
    first = cls * per
    s = (cls + 1) * per * Q_BLOCK
    r = qb - first
    diag = _diag_causal(qb, first, per)
    lo = lax.broadcasted_iota(jnp.int32, (Q_BLOCK, LANES), 1) < HD_C
    zero = jnp.zeros((), BF16)
    gsz = H_C // H_C_KV

    def qk(head, e):
        g = head // gsz
        grp = q_ref[:, head * LANES:(head + 1) * LANES]
        qm = jnp.where(lo, grp, zero) if e == 0 else jnp.where(lo, zero, grp)
        addends = _near_addends(near_ref, head, r, first, per)
        for j in range(per):
            addends[first + j] = addends[first + j] + diag[:, j * Q_BLOCK:(j + 1) * Q_BLOCK]
        return _add_to_blocks(_dot(qm, ckt_ref[g * LANES:(g + 1) * LANES, :s]), addends)

    def pv(head, logits):
        g = head // gsz
        p, inv = _softmax_parts(logits)
        return _dot(p, cv_ref[:s, g * LANES:(g + 1) * LANES]) * inv

    held = {}

    def finish(head, e, logits):
        o = pv(head, logits)
        if e == 0:
            held["o"] = o
        else:
            o = held.pop("o") - lam * o
            o = _rms(o, subg_ref[...]) * (1.0 - lam_init)
            o_ref[:, head * LANES:(head + 1) * LANES] = o.astype(BF16)

    _pipelined([(functools.partial(qk, h, e), functools.partial(finish, h, e)) for h in range(H_C) for e in range(2)])


def _attn_odd_kernel(lam_ref, q_ref, ckt_ref, cv_ref, near_ref, subg_ref, o_ref, *, t, n_cls, lam_init):
    qb = pl.program_id(1)
    per = (t // Q_BLOCK) // n_cls
    lam = lam_ref[0]
    for c in range(n_cls):
        @pl.when(qb // per == c)
        def _(c=c):
            _attn_odd_body(c, per, qb, lam, lam_init, q_ref, ckt_ref, cv_ref, near_ref, subg_ref, o_ref)


def _attn_odd(lam, q, ckt, cv, near, subg, n_batch, t, lam_init):
    nqb = t // Q_BLOCK
    qrow = lambda cols: pl.BlockSpec((Q_BLOCK, cols), lambda b, i: (b * nqb + i, 0))
    return pl.pallas_call(
        functools.partial(_attn_odd_kernel, t=t, n_cls=_key_classes(t, ODD_CLASSES), lam_init=lam_init),
        grid=(n_batch, nqb),
        in_specs=[pl.BlockSpec(memory_space=pltpu.SMEM), qrow(1024),
                  pl.BlockSpec((None, 256, t), lambda b, i: (b, 0, 0)), pl.BlockSpec((t, 256), lambda b, i: (b, 0)),
                  _const_spec(near.shape), _const_spec(subg.shape)],
        out_specs=qrow(1024), out_shape=jax.ShapeDtypeStruct((n_batch * t, 1024), BF16),
        compiler_params=_params(("parallel", "arbitrary"), 40 * 2 ** 20),
        name="attn_odd_prompt",
    )(lam, q, ckt, cv, near, subg)


PAGES_PER_STEP = 32


def _page_specs(block, layer, n):
    nd = len(block)

    def spec(j):
        return pl.BlockSpec((None, None) + block,
                            lambda b, g, pt: (layer, pt[b, g * n + j]) + (0,) * nd)

    return [spec(j) for j in range(n)]


def _dec_scores_kernel(pt_ref, iq_ref, iw_ref, *refs, n):
    pages, out_ref = refs[:n], refs[n]
    iq = iq_ref[...]
    iw = iw_ref[...]
    for j in range(n):
        sc = jnp.maximum(_dot(iq, pages[j][...].astype(BF16)), 0.0) * iw
        out_ref[:, j * LANES:(j + 1) * LANES] = jnp.sum(sc, axis=0, keepdims=True)


def _dec_scores(page_table, iq_s, iw_s, cache_idx_t, layer):
    nb, npg = page_table.shape
    n = PAGES_PER_STEP if npg % PAGES_PER_STEP == 0 else 1
    page = cache_idx_t.shape[3]
    assert page == LANES
    grid_spec = pltpu.PrefetchScalarGridSpec(
        num_scalar_prefetch=1, grid=(nb, npg // n),
        in_specs=[pl.BlockSpec((None, 8, HD_IDX), lambda b, g, pt: (b, 0, 0)),
                  pl.BlockSpec((None, 8, 1), lambda b, g, pt: (b, 0, 0))]
        + _page_specs((HD_IDX, page), layer, n),
        out_specs=pl.BlockSpec((None, 1, n * page), lambda b, g, pt: (b, 0, g)))
    return pl.pallas_call(
        functools.partial(_dec_scores_kernel, n=n), grid_spec=grid_spec,
        out_shape=jax.ShapeDtypeStruct((nb, 1, npg * page), F32),
        compiler_params=_params(("parallel", "arbitrary"), 32 * 2 ** 20),
        name="decode_index_scores",
    )(page_table, iq_s, iw_s, *([cache_idx_t] * n))


def _dec_select_kernel(sc_ref, iq_ref, ik_ref, iw_ref, neg_ref, negself_ref, *, topk):
    rows, past = sc_ref.shape
    iq = iq_ref[...].astype(F32)
    ik = ik_ref[...]
    iw = iw_ref[...]
    self_score = jnp.zeros((rows, 1), F32)
    for hh in range(H_IDX):
        s = jnp.sum(iq[:, hh * HD_IDX:(hh + 1) * HD_IDX] * ik, axis=1, keepdims=True)
        self_score = self_score + jnp.maximum(s, 0.0) * iw[:, hh:hh + 1]
    lane = lax.broadcasted_iota(jnp.int32, (rows, LANES), 1)
    tail = jnp.where(lane == 0, self_score, NEG_INF)
    score = jnp.concatenate([sc_ref[...], tail], axis=1)
    col = lax.broadcasted_iota(jnp.int32, score.shape, 1)
    neg = jnp.where(col <= past, _topk_neg_mask(score, col, topk, 1), NEG_INF)
    neg_ref[...] = neg[:, :past]
    negself_ref[...] = neg[:, past:]


def _dec_select(scores, iq, ik, iw, topk):
    rows, past = scores.shape
    return pl.pallas_call(
        functools.partial(_dec_select_kernel, topk=topk),
        out_shape=[jax.ShapeDtypeStruct((rows, past), F32), jax.ShapeDtypeStruct((rows, LANES), F32)],
        compiler_params=pltpu.CompilerParams(vmem_limit_bytes=_vmem_limit(48 * 2 ** 20)),
        name="decode_topk_select",
    )(scores, iq, ik, iw)


def _flash_update(m_ref, s_ref, acc_ref, logits, pv_fn):
    m_old = m_ref[...]
    m_new = jnp.maximum(m_old, jnp.max(logits, axis=1, keepdims=True))
    m_safe = jnp.where(m_new == NEG_INF, 0.0, m_new)
    alpha = jnp.exp2(m_old - m_safe)
    p = jnp.exp2(logits - m_safe)
    s_ref[...] = alpha * s_ref[...] + jnp.sum(p, axis=1, keepdims=True)
    acc_ref[...] = alpha * acc_ref[...] + pv_fn(p.astype(BF16))
    m_ref[...] = m_new


def _flash_finish(m_ref, s_ref, acc_ref, l_self, v_self):
    m_old = m_ref[...]
    m_new = jnp.maximum(m_old, l_self)
    m_safe = jnp.where(m_new == NEG_INF, 0.0, m_new)
    alpha = jnp.exp2(m_old - m_safe)
    p = jnp.exp2(l_self - m_safe)
    return (alpha * acc_ref[...] + p * v_self) / (alpha * s_ref[...] + p)


def _add_near_last_page(logits, near, n):
    if n == 1:
        return logits + near
    return jnp.concatenate([logits[:, :(n - 1) * LANES], logits[:, (n - 1) * LANES:] + near], axis=1)


def _dec_even_kernel(pt_ref, neg_ref, negself_ref, aq_ref, qlat_ref, kvn_ref, latn_ref, near_ref, bself_ref,
                     *refs, n):
    kv_pages, lat_pages = refs[:n], refs[n:2 * n]
    oa_ref, ol_ref = refs[2 * n:2 * n + 2]
    ma_ref, sa_ref, acca_ref, mb_ref, sb_ref, accb_ref = refs[2 * n + 2:]
    g = pl.program_id(1)
    last = pl.num_programs(1) - 1

    @pl.when(g == 0)
    def _():
        ma_ref[...] = jnp.full(ma_ref.shape, NEG_INF, F32)
        mb_ref[...] = jnp.full(mb_ref.shape, NEG_INF, F32)
        sa_ref[...] = jnp.zeros(sa_ref.shape, F32)
        sb_ref[...] = jnp.zeros(sb_ref.shape, F32)
        acca_ref[...] = jnp.zeros(acca_ref.shape, F32)
        accb_ref[...] = jnp.zeros(accb_ref.shape, F32)

    aq = aq_ref[...]
    qlat = qlat_ref[...]
    kv = [p[...].astype(BF16) for p in kv_pages]
    la = jnp.concatenate([_dot(aq, kvp[:LANES]) for kvp in kv], axis=1) + neg_ref[...]
    la = _add_near_last_page(la, jnp.where(g == last, near_ref[...], 0.0), n)

    def pv_a(p):
        return sum(_dot_nt(p[:, j * LANES:(j + 1) * LANES], kv[j][LANES:]) for j in range(n))

    _flash_update(ma_ref, sa_ref, acca_ref, la, pv_a)
    lat = [p[...].astype(BF16) for p in lat_pages]
    zrows = jnp.zeros((256 - KV_LORA - ROPE_B, LANES), BF16)
    lb = jnp.concatenate([_dot(qlat, jnp.concatenate([lp, zrows], axis=0)) for lp in lat], axis=1)

    def pv_b(p):
        return sum(_dot_nt(p[:, j * LANES:(j + 1) * LANES], lat[j][:KV_LORA]) for j in range(n))

    _flash_update(mb_ref, sb_ref, accb_ref, lb, pv_b)

    @pl.when(g == last)
    def _():
        kvn = kvn_ref[...]
        l_self = (jnp.sum(aq.astype(F32) * kvn[:, :LANES], axis=1, keepdims=True) + bself_ref[...]
                  + negself_ref[:, 0:1])
        oa = _flash_finish(ma_ref, sa_ref, acca_ref, l_self, kvn[:, LANES:])
        rowi = lax.broadcasted_iota(jnp.int32, oa.shape, 0)
        oa = jnp.where(rowi < H_A // H_A_KV, oa, pltpu.roll(oa, HD_A, 1))
        oa_ref[...] = oa[:, :HD_A]
        latn = latn_ref[...]
        lb_self = jnp.sum(qlat.astype(F32) * latn, axis=1, keepdims=True)
        ol_ref[...] = _flash_finish(mb_ref, sb_ref, accb_ref, lb_self, latn[:, :KV_LORA]).astype(BF16)


def _dec_even(page_table, neg, negself, aq_s, qlat_s, kv_new, lat_new, near_s, bself, cache_kv_t, cache_lat_t,
              layer):
    nb, npg = page_table.shape
    n = PAGES_PER_STEP if npg % PAGES_PER_STEP == 0 else 1
    page = cache_kv_t.shape[3]
    per_seq = lambda shape: pl.BlockSpec((None,) + shape, lambda b, g, pt: (b,) + (0,) * len(shape))
    const = lambda shape: pl.BlockSpec(shape, lambda b, g, pt: (0,) * len(shape))
    grid_spec = pltpu.PrefetchScalarGridSpec(
        num_scalar_prefetch=1, grid=(nb, npg // n),
        in_specs=[pl.BlockSpec((None, 1, n * page), lambda b, g, pt: (b, 0, g)), per_seq((1, LANES)),
                  per_seq((H_A, LANES)), per_seq((H_B, 256)), per_seq((1, 256)), per_seq((1, 256)),
                  const((H_A, LANES)), const((H_A, 1))]
        + _page_specs((256, page), layer, n) + _page_specs((KV_LORA + ROPE_B, page), layer, n),
        out_specs=[per_seq((H_A, HD_A)), per_seq((H_B, KV_LORA))],
        scratch_shapes=[pltpu.VMEM((H_A, 1), F32), pltpu.VMEM((H_A, 1), F32), pltpu.VMEM((H_A, LANES), F32),
                        pltpu.VMEM((H_B, 1), F32), pltpu.VMEM((H_B, 1), F32), pltpu.VMEM((H_B, KV_LORA), F32)])
    return pl.pallas_call(
        functools.partial(_dec_even_kernel, n=n), grid_spec=grid_spec,
        out_shape=[jax.ShapeDtypeStruct((nb, H_A, HD_A), F32), jax.ShapeDtypeStruct((nb, H_B, KV_LORA), BF16)],
        compiler_params=_params(("parallel", "arbitrary"), 32 * 2 ** 20),
        name="decode_attn_even",
    )(page_table, neg, negself, aq_s, qlat_s, kv_new, lat_new, near_s, bself,
      *([cache_kv_t] * n), *([cache_lat_t] * n))


def _dec_odd_kernel(pt_ref, lam_ref, q_ref, kvn_ref, near_ref, bself_ref, subg_ref, *refs, n, lam_init):
    pages = refs[:n]
    o_ref = refs[n]
    m_ref, s_ref, acc_ref = refs[n + 1:]
    g = pl.program_id(1)
    last = pl.num_programs(1) - 1
    page = pages[0].shape[0] // (2 * H_C_KV)
    width = 2 * HD_C

    @pl.when(g == 0)
    def _():
        m_ref[...] = jnp.full(m_ref.shape, NEG_INF, F32)
        s_ref[...] = jnp.zeros(s_ref.shape, F32)
        acc_ref[...] = jnp.zeros(acc_ref.shape, F32)

    q = q_ref[...]
    row_g0 = lax.broadcasted_iota(jnp.int32, (2 * H_C, width), 0) % H_C < H_C // H_C_KV
    part = lambda p, r: p[pl.ds(r, page, stride=2 * H_C_KV), :].astype(BF16)
    logits = jnp.concatenate(
        [_dot_nt(q[:, :width], part(p, 0)) + _dot_nt(q[:, width:], part(p, 1)) for p in pages], axis=1)
    logits = _add_near_last_page(logits, jnp.where(g == last, near_ref[...], 0.0), n)

    def pv(p):
        v0 = sum(_dot(p[:, j * LANES:(j + 1) * LANES], part(pages[j], 2)) for j in range(n))
        v1 = sum(_dot(p[:, j * LANES:(j + 1) * LANES], part(pages[j], 3)) for j in range(n))
        return jnp.where(row_g0, v0, v1)

    _flash_update(m_ref, s_ref, acc_ref, logits, pv)

    @pl.when(g == last)
    def _():
        kvn = kvn_ref[...]
        l_self = jnp.sum(q.astype(F32) * kvn[:, :2 * width], axis=1, keepdims=True) + bself_ref[...]
        v_self = jnp.where(row_g0, kvn[:, 2 * width:3 * width], kvn[:, 3 * width:])
        o = _flash_finish(m_ref, s_ref, acc_ref, l_self, v_self)
        d = o[:H_C] - lam_ref[0] * o[H_C:]
        o_ref[...] = _rms(d, subg_ref[...]) * (1.0 - lam_init)


def _dec_odd(page_table, lam, q_s, kv_new, near_s, bself, subg, cache_kv_r, layer, lam_init):
    nb, npg = page_table.shape
    n = PAGES_PER_STEP if npg % PAGES_PER_STEP == 0 else 1
    rows, width = cache_kv_r.shape[2:]
    per_seq = lambda shape: pl.BlockSpec((None,) + shape, lambda b, g, pt: (b,) + (0,) * len(shape))
    const = lambda shape: pl.BlockSpec(shape, lambda b, g, pt: (0,) * len(shape))
    grid_spec = pltpu.PrefetchScalarGridSpec(
        num_scalar_prefetch=1, grid=(nb, npg // n),
        in_specs=[pl.BlockSpec(memory_space=pltpu.SMEM), per_seq((2 * H_C, H_C_KV * width)),
                  per_seq((1, 2 * H_C_KV * width)), const((2 * H_C, LANES)), const((2 * H_C, 1)), const((1, width))]
        + _page_specs((rows, width), layer, n),
        out_specs=per_seq((H_C, width)),
        scratch_shapes=[pltpu.VMEM((2 * H_C, 1), F32), pltpu.VMEM((2 * H_C, 1), F32),
                        pltpu.VMEM((2 * H_C, width), F32)])
    return pl.pallas_call(
        functools.partial(_dec_odd_kernel, n=n, lam_init=lam_init), grid_spec=grid_spec,
        out_shape=jax.ShapeDtypeStruct((nb, H_C, width), F32),
        compiler_params=_params(("parallel", "arbitrary"), 32 * 2 ** 20),
        name="decode_attn_odd",
    )(page_table, lam, q_s, kv_new, near_s, bself, subg, *([cache_kv_r] * n))


def _t5_bias(table, rel):
    n = jnp.maximum(rel, 0)
    max_exact = NUM_BUCKETS // 2
    nf = jnp.maximum(n, 1).astype(F32)
    large = max_exact + (jnp.log(nf / max_exact) / math.log(MAX_DISTANCE / max_exact)
                         * (NUM_BUCKETS - max_exact)).astype(jnp.int32)
    bucket = jnp.where(n < max_exact, n, jnp.minimum(large, NUM_BUCKETS - 1))
    hit = bucket[..., None, None] == jnp.arange(NUM_BUCKETS, dtype=jnp.int32)[:, None]
    return jnp.sum(jnp.where(hit, table.astype(F32), 0.0), axis=-2)


def _far_distance_is_constant(first_far, max_rel):
    n = np.arange(first_far, max_rel + 1, dtype=np.float64)
    large = NUM_BUCKETS // 2 + np.floor(np.log(n / (NUM_BUCKETS // 2)) / math.log(MAX_DISTANCE / (NUM_BUCKETS // 2))
                                        * (NUM_BUCKETS - NUM_BUCKETS // 2) - 1e-3)
    return bool(np.all(large >= NUM_BUCKETS - 1))


def _near_tiles(table, per):
    r = jnp.arange(Q_BLOCK, dtype=jnp.int32)[:, None]
    c = jnp.arange(Q_BLOCK, dtype=jnp.int32)[None, :]
    far = table[NUM_BUCKETS - 1].astype(F32)
    before = jnp.transpose(_t5_bias(table, Q_BLOCK + r - c) - far, (2, 0, 1)) * LOG2E
    diag = jnp.transpose(_t5_bias(table, r - c) - far, (2, 0, 1)) * LOG2E
    z = jnp.zeros((table.shape[1], per - 1, Q_BLOCK, Q_BLOCK), F32)
    return jnp.concatenate([z, before[:, None], diag[:, None], z], axis=1)


def _near_rows(table, page):
    rel = page - jnp.arange(page, dtype=jnp.int32)
    far = table[NUM_BUCKETS - 1].astype(F32)
    return ((_t5_bias(table, rel) - far) * LOG2E).T, ((_t5_bias(table, jnp.zeros((1,), jnp.int32)) - far) * LOG2E).T


def _rope_tables(pos, rows):
    half = ROPE_B // 2
    inv = ROPE_THETA ** (-jnp.arange(half, dtype=F32) / half)
    ang = pos.astype(F32)[:, None] * inv[None, :]
    cos, sin = jnp.cos(ang), jnp.sin(ang)
    z = jnp.zeros_like(cos)
    pad = jnp.zeros((pos.shape[0], LANES - ROPE_B), F32)
    tabs = [jnp.concatenate([cos, cos, pad], 1), jnp.concatenate([sin, sin, pad], 1),
            jnp.concatenate([-sin, z, pad], 1), jnp.concatenate([z, sin, pad], 1)]
    return [jnp.broadcast_to(t, (rows, LANES)) if t.shape[0] != rows else t for t in tabs]


def _prep_ab_weights(w_in, w_uq, w_uk, w_uv):
    d = w_in.shape[0]
    o = np.cumsum([0, H_A * HD_A, H_A_KV * HD_A, H_A_KV * HD_A, H_IDX * HD_IDX, HD_IDX, H_IDX, Q_LORA, KV_LORA,
                   ROPE_B])
    sec = lambda i: w_in[:, o[i]:o[i + 1]]
    w1 = jnp.concatenate([sec(0), sec(1), sec(2), sec(3), sec(6), sec(7), sec(8), sec(5),
                          jnp.zeros((d, GRP_IK - GRP_IW - H_IDX), w_in.dtype), sec(4)], axis=1).astype(BF16)
    assert w1.shape[1] == AB_COLS
    uq = w_uq.reshape(Q_LORA, H_B, NOPE_B + ROPE_B)
    nope = uq[:, :, :NOPE_B].reshape(Q_LORA, H_B * NOPE_B)
    r = uq[:, :, NOPE_B:]
    half = ROPE_B // 2
    rot = jnp.concatenate([-r[:, :, half:], r[:, :, :half]], axis=-1)
    padl = lambda a: jnp.pad(a, ((0, 0), (0, 0), (0, LANES - ROPE_B))).reshape(Q_LORA, H_B * LANES)
    w2 = jnp.concatenate([nope, padl(r), padl(rot)], axis=1).astype(BF16)
    eye = jnp.eye(H_B, dtype=w_uk.dtype)
    wuk = jnp.einsum("chn,hg->hngc", w_uk, eye).reshape(H_B * NOPE_B, H_B * KV_LORA).astype(BF16)
    wv = jnp.transpose(w_uv, (1, 0, 2))
    zl = jnp.zeros_like(wv)
    even = (jnp.arange(H_B) % 2 == 0)[:, None, None]
    wuv = jnp.where(even, jnp.concatenate([wv, zl], -1), jnp.concatenate([zl, wv], -1)).astype(BF16)
    wuv_bd = jnp.einsum("chv,hg->hcgv", w_uv, eye).reshape(H_B * KV_LORA, H_B * V_B).astype(BF16)
    return w1, w2, wuk, wuv, wuv_bd


def _lambda(lambda_qk, layer):
    lam_init = 0.8 - 0.6 * math.exp(-0.3 * layer)
    lq = lambda_qk.astype(F32)
    lam = jnp.exp(jnp.sum(lq[0] * lq[1])) - jnp.exp(jnp.sum(lq[2] * lq[3])) + lam_init
    return lam.reshape(1), lam_init


def _run_prompt(x, P):
    nb, t, d = x.shape
    assert t % Q_BLOCK == 0
    m = nb * t
    topk = min(TOPK_MAX, t // 4)
    xf = x.reshape(m, d)
    tabs = _rope_tables(jnp.arange(t, dtype=jnp.int32), t)
    a_kv_l, a_idx_l, b_lat_l, c_kv_l = [], [], [], []
    depth = P["attn_norm"].shape[0]
    for l in range(depth):
        last = l == depth - 1
        if l % 2 == 0:
            e = l // 2
            w1, w2, wuk, wuv, _ = P["ab"][e]
            (aq, akv, iq, ik, iw, qlat, blat, av2, latc, ik2, akt, iwt, latt) = _ab_proj(
                xf, P["attn_norm"][l][None], w1, P["g_q"][e][None], w2, wuk, P["g_kv"][e][None], tabs, nb, t, True)
            o = _attn_even(aq, iq, iwt, qlat, akt, av2, ik2, latt, latc, P["near_a"], wuv, nb, t, topk)
            wo = P["w_o_ab"][e]
            a_kv_l.append(akv.reshape(nb, t, 2, H_A_KV, HD_A))
            a_idx_l.append(ik.reshape(nb, t, HD_IDX))
            b_lat_l.append(blat.reshape(nb, t, KV_LORA + ROPE_B))
        else:
            j = l // 2
            lam, lam_init = _lambda(P["lambda_qk"][j], l)
            q, ckv, ckt, cv = _c_proj(xf, P["attn_norm"][l][None], P["w_in_c"][j], nb, t, True)
            o = _attn_odd(lam, q, ckt, cv, P["near_c"], P["sub_norm_c"][j][None], nb, t, lam_init)
            wo = P["w_o_c"][j]
            c_kv_l.append(ckv.reshape(nb, t, 2, H_C_KV, 2 * HD_C))
        xf = _out_ffn(xf, o, wo, P["ffn_norm"][l][None], P["w_gate_up"][l], P["w_down"][l], P["final_norm"][None],
                      last)
    return (xf.reshape(nb, t, d), jnp.stack(a_kv_l), jnp.stack(a_idx_l), jnp.stack(b_lat_l), jnp.stack(c_kv_l))


def _run_sample(x, P, caches, page_table):
    cache_a_kv, cache_a_idx, cache_b_latent, cache_c_kv = caches
    nb, t, d = x.shape
    assert t == 1
    npg = page_table.shape[1]
    page = cache_a_idx.shape[2]
    past = npg * page
    topk = min(TOPK_MAX, (past + t) // 4)
    assert _far_distance_is_constant(page + 1, past)
    xf = x.reshape(nb, d)
    tabs = _rope_tables(jnp.full((1,), past, jnp.int32), nb)
    idx_t = jnp.transpose(cache_a_idx, (0, 1, 3, 2))
    kv_t = jnp.transpose(cache_a_kv, (0, 1, 3, 4, 5, 2)).reshape(cache_a_kv.shape[:2] + (-1, page))
    lat_t = jnp.transpose(cache_b_latent, (0, 1, 3, 2))
    ckv_r = cache_c_kv.reshape(cache_c_kv.shape[:2] + (-1, cache_c_kv.shape[-1]))
    a_kv_l, a_idx_l, b_lat_l, c_kv_l = [], [], [], []
    depth = P["attn_norm"].shape[0]
    gsz_a = H_A // H_A_KV
    head_lo_a = (jnp.arange(H_A) < gsz_a)[None, :, None]
    for l in range(depth):
        last = l == depth - 1
        if l % 2 == 0:
            e = l // 2
            w1, w2, wuk, _, wuv_bd = P["ab"][e]
            aq, akv, iq, ik, iw, qlat, blat = _ab_proj(
                xf, P["attn_norm"][l][None], w1, P["g_q"][e][None], w2, wuk, P["g_kv"][e][None], tabs, nb, t, False)
            iq_s = jnp.pad(iq.reshape(nb, H_IDX, HD_IDX), ((0, 0), (0, 8 - H_IDX), (0, 0)))
            iw_s = jnp.pad(iw, ((0, 0), (0, 8 - H_IDX)))[:, :, None]
            scores = _dec_scores(page_table, iq_s, iw_s, idx_t, e)
            neg, negself = _dec_select(scores.reshape(nb, past), iq, ik, iw, topk)
            aq3 = aq.reshape(nb, H_A, HD_A)
            zq = jnp.zeros_like(aq3)
            aq_s = jnp.where(head_lo_a, jnp.concatenate([aq3, zq], -1), jnp.concatenate([zq, aq3], -1))
            lat_new = jnp.pad(blat, ((0, 0), (0, 256 - blat.shape[1])))
            oa, olat = _dec_even(page_table, neg.reshape(nb, 1, past), negself.reshape(nb, 1, LANES), aq_s,
                                 qlat.reshape(nb, H_B, 256), akv.reshape(nb, 1, 256), lat_new.reshape(nb, 1, 256),
                                 P["near_a_s"], P["bself_a"], kv_t, lat_t, e)
            ob = _mm(olat.reshape(nb, H_B * KV_LORA), wuv_bd, BF16)
            o = jnp.concatenate([oa.reshape(nb, H_A * HD_A).astype(BF16), ob], axis=1)
            wo = P["w_o_ab"][e]
            a_kv_l.append(akv.reshape(nb, t, 2, H_A_KV, HD_A))
            a_idx_l.append(ik.reshape(nb, t, HD_IDX))
            b_lat_l.append(blat.reshape(nb, t, KV_LORA + ROPE_B))
        else:
            j = l // 2
            lam, lam_init = _lambda(P["lambda_qk"][j], l)
            q, ckv = _c_proj(xf, P["attn_norm"][l][None], P["w_in_c"][j], nb, t, False)
            q4 = q.reshape(nb, H_C, 2, HD_C)
            sel = ((jnp.arange(H_C)[:, None, None, None] // (H_C // H_C_KV) == jnp.arange(H_C_KV)[None, None, :, None])
                   & (jnp.arange(2)[None, :, None, None] == jnp.arange(2)[None, None, None, :]))
            q_s = jnp.where(sel[None, :, :, :, :, None], q4[:, :, :, None, None, :], jnp.zeros((), q.dtype))
            q_s = jnp.transpose(q_s, (0, 2, 1, 3, 4, 5)).reshape(nb, 2 * H_C, H_C_KV * 2 * HD_C)
            oc = _dec_odd(page_table, lam, q_s, ckv.reshape(nb, 1, -1), P["near_c_s"], P["bself_c"],
                          P["sub_norm_c"][j][None], ckv_r, j, lam_init)
            o = oc.reshape(nb, H_C * 2 * HD_C).astype(BF16)
            wo = P["w_o_c"][j]
            c_kv_l.append(ckv.reshape(nb, t, 2, H_C_KV, 2 * HD_C))
        xf = _out_ffn(xf, o, wo, P["ffn_norm"][l][None], P["w_gate_up"][l], P["w_down"][l], P["final_norm"][None],
                      last)
    return (xf.reshape(nb, t, d), jnp.stack(a_kv_l), jnp.stack(a_idx_l), jnp.stack(b_lat_l), jnp.stack(c_kv_l))


def kernel(x_prompt, x_sample, cache_a_kv, cache_a_idx, cache_b_latent, cache_c_kv, page_table, attn_norm, w_in_ab,
           w_uq, g_q, g_kv, w_uk, w_uv, w_o_ab, rel_bias, w_in_c, lambda_qk, sub_norm_c, w_o_c, ffn_norm, w_gate_up,
           w_down, final_norm):
    page = cache_a_idx.shape[2]
    assert page == LANES
    assert _far_distance_is_constant(Q_BLOCK + 1, max(x_prompt.shape[1], Q_BLOCK + 1))
    tab_a, tab_c = rel_bias[:, :H_A], rel_bias[:, H_A:]
    nqb = x_prompt.shape[1] // Q_BLOCK
    per_a = nqb // _key_classes(x_prompt.shape[1], EVEN_CLASSES)
    per_c = nqb // _key_classes(x_prompt.shape[1], ODD_CLASSES)
    near_a_s, bself_a = _near_rows(tab_a, page)
    near_c_s, bself_c = _near_rows(tab_c, page)
    P = dict(
        attn_norm=attn_norm, g_q=g_q, g_kv=g_kv, ffn_norm=ffn_norm, final_norm=final_norm, lambda_qk=lambda_qk,
        sub_norm_c=sub_norm_c,
        ab=[_prep_ab_weights(w_in_ab[e], w_uq[e], w_uk[e], w_uv[e]) for e in range(w_in_ab.shape[0])],
        w_o_ab=w_o_ab.astype(BF16), w_in_c=w_in_c.astype(BF16), w_o_c=w_o_c.astype(BF16),
        w_gate_up=w_gate_up.astype(BF16), w_down=w_down.astype(BF16),
        near_a=_near_tiles(tab_a, per_a), near_c=_near_tiles(tab_c, per_c),
        near_a_s=near_a_s, bself_a=bself_a,
        near_c_s=jnp.concatenate([near_c_s, near_c_s], 0), bself_c=jnp.concatenate([bself_c, bself_c], 0),
    )
    y_p, akv_p, aidx_p, blat_p, ckv_p = _run_prompt(x_prompt, P)
    y_s, akv_s, aidx_s, blat_s, ckv_s = _run_sample(
        x_sample, P, (cache_a_kv, cache_a_idx, cache_b_latent, cache_c_kv), page_table)
    return (y_p, y_s, akv_p, aidx_p, blat_p, ckv_p, akv_s, aidx_s, blat_s, ckv_s)
```
